```python
import jax, jax.numpy as jnp
from jax import lax
import numpy as np

D_MODEL = 2048
BATCH = 16
SEQ = 2048
DEPTH = 2

CTX_LEN = 256
GRID_W = 64
MIX_WIDTH = D_MODEL // 2
HG_HEAD_DIM = 128
HG_HEADS = MIX_WIDTH // HG_HEAD_DIM
HG_CHUNK = 64
ATTN_HEAD_DIM = 64
ATTN_HEADS = MIX_WIDTH // ATTN_HEAD_DIM
ATTN_KV_HEADS = ATTN_HEADS // 8
ATTN_GROUP = ATTN_HEADS // ATTN_KV_HEADS
KV_WIDTH = ATTN_KV_HEADS * ATTN_HEAD_DIM
WINDOW = 128
ATTN_BLOCK = 128
ROPE_THETA = 10000.0
D_FF = ((8 * D_MODEL // 3 + 255) // 256) * 256
N_EXPERTS = 8
TOP_K = 2
D_EXPERT = 7 * D_MODEL // 2
EPS = 1e-6
IN_WIDTHS = (MIX_WIDTH, MIX_WIDTH, MIX_WIDTH, MIX_WIDTH, MIX_WIDTH, MIX_WIDTH, KV_WIDTH, KV_WIDTH, D_MODEL, D_MODEL)
N_IN = sum(IN_WIDTHS)

kernel_name = 'hybrid_hgrn2_swa_moe_flow_block'


def _rms(x):
    xf = x.astype(jnp.float32)
    return (xf * lax.rsqrt(jnp.mean(xf * xf, axis=-1, keepdims=True) + EPS)).astype(x.dtype)


def rmsnorm(x, g):
    return _rms(x) * g


def modulate(h, shift, scale):
    return h * (1.0 + scale) + shift


def heads(t, n):
    return t.reshape(*t.shape[:-1], n, t.shape[-1] // n)


def swiglu(h, wg, wu, wd):
    return (jax.nn.silu(h @ wg) * (h @ wu)) @ wd


def moe_swiglu(h, router, wg, wu, wd):
    logits = (h @ router).astype(jnp.float32)
    top_v, top_i = lax.top_k(logits, TOP_K)
    w = jax.nn.softmax(top_v, axis=-1)
    combine = jnp.sum(jax.nn.one_hot(top_i, N_EXPERTS, dtype=jnp.float32) * w[..., None], axis=-2).astype(h.dtype)
    out = jnp.zeros_like(h)
    for e in range(N_EXPERTS):
        out = out + combine[..., e:e + 1] * swiglu(h, wg[e], wu[e], wd[e])
    return out


def channel_mixer(h, layer, ffn_w_gate, ffn_w_up, ffn_w_down, moe_router, moe_w_gate, moe_w_up, moe_w_down):
    i = layer // 2
    if layer % 2 == 0:
        return swiglu(h, ffn_w_gate[i], ffn_w_up[i], ffn_w_down[i])
    return moe_swiglu(h, moe_router[i], moe_w_gate[i], moe_w_up[i], moe_w_down[i])


def rope_tables(length):
    rows = length // GRID_W
    row = jnp.repeat(jnp.arange(rows, dtype=jnp.float32), GRID_W)
    col = jnp.tile(jnp.arange(GRID_W, dtype=jnp.float32), rows)
    n_freq = ATTN_HEAD_DIM // 4
    inv = ROPE_THETA ** (-jnp.arange(n_freq, dtype=jnp.float32) / n_freq)
    ang = jnp.stack([row[:, None] * inv, col[:, None] * inv], axis=1)
    return jnp.cos(ang), jnp.sin(ang)


def apply_rope(t, cos, sin):
    tr = t.reshape(*t.shape[:-1], 2, 2, ATTN_HEAD_DIM // 4)
    x1, x2 = tr[..., 0, :], tr[..., 1, :]
    cs, sn = cos[None, :, None], sin[None, :, None]
    out = jnp.stack([x1 * cs - x2 * sn, x2 * cs + x1 * sn], axis=-2)
    return out.reshape(t.shape).astype(t.dtype)


def forget_gate(f_raw, lb):
    f = lb + (1.0 - lb) * jax.nn.sigmoid(f_raw.astype(jnp.float32))
    return heads(jnp.log(f), HG_HEADS), heads(1.0 - f, HG_HEADS)


def lower_bound(raw, layer):
    p = jax.nn.softmax(raw.astype(jnp.float32), axis=0)
    return (jnp.cumsum(p, axis=0) - p[0])[layer]


def hgrn2_chunk_scan(q, k, v, log_f, s0):
    b_, length, h, _ = q.shape
    n_chunks = length // HG_CHUNK

    def chunks(t):
        return t.reshape(b_, n_chunks, HG_CHUNK, h, t.shape[-1]).transpose(1, 0, 3, 2, 4)

    lower = jnp.tril(jnp.ones((HG_CHUNK, HG_CHUNK), dtype=bool))[:, :, None]

    def step(state, inp):
        qc, kc, vc, gc = inp
        b = jnp.cumsum(gc, axis=2)
        o_inter = jnp.einsum('bhtd,bhdv->bhtv', qc * jnp.exp(b), state)
        decay = jnp.exp(jnp.where(lower, b[:, :, :, None, :] - b[:, :, None, :, :], -jnp.inf))
        scores = jnp.einsum('bhtd,bhsd,bhtsd->bhts', qc, kc, decay)
        o = o_inter + jnp.einsum('bhts,bhsv->bhtv', scores, vc)
        b_last = b[:, :, -1:, :]
        state = jnp.exp(b_last[:, :, 0, :])[..., None] * state + jnp.einsum('bhsd,bhsv->bhdv', kc * jnp.exp(b_last - b), vc)
        return state, o

    s_fin, o = lax.scan(step, s0, (chunks(q), chunks(k), chunks(v), chunks(log_f)))
    o = o.transpose(1, 0, 3, 2, 4).reshape(b_, length, h, v.shape[-1])
    return o, s_fin


def _orient(ts, reverse):
    return tuple((jnp.flip(t, axis=1) if reverse else t).astype(jnp.float32) for t in ts)


def hgrn2_direction(ctx_in, lat_in, reverse):
    ctx_in, lat_in = _orient(ctx_in, reverse), _orient(lat_in, reverse)
    s0 = jnp.zeros((ctx_in[0].shape[0], HG_HEADS, HG_HEAD_DIM, HG_HEAD_DIM), jnp.float32)
    o_ctx, s_ctx = hgrn2_chunk_scan(*ctx_in, s0)
    o_lat, _ = hgrn2_chunk_scan(*lat_in, s_ctx)
    if reverse:
        o_ctx, o_lat = jnp.flip(o_ctx, axis=1), jnp.flip(o_lat, axis=1)
    return o_ctx, o_lat


def window_attention(q, k, v, k_ctx, v_ctx, sink):
    b_, length = q.shape[:2]
    nb = length // ATTN_BLOCK
    n_side = -(-WINDOW // ATTN_BLOCK)
    kw = (2 * n_side + 1) * ATTN_BLOCK
    qb = q.reshape(b_, nb, ATTN_BLOCK, ATTN_KV_HEADS, ATTN_GROUP, ATTN_HEAD_DIM)
    pad = ((0, 0), (n_side * ATTN_BLOCK, n_side * ATTN_BLOCK), (0, 0), (0, 0))

    def banded(t):
        tp = jnp.pad(t, pad).reshape(b_, nb + 2 * n_side, ATTN_BLOCK, ATTN_KV_HEADS, ATTN_HEAD_DIM)
        return jnp.concatenate([tp[:, s:s + nb] for s in range(2 * n_side + 1)], axis=2)

    kb, vb = banded(k), banded(v)
    start = jnp.arange(nb)[:, None] * ATTN_BLOCK
    q_pos = start + jnp.arange(ATTN_BLOCK)[None, :]
    k_pos = start - n_side * ATTN_BLOCK + jnp.arange(kw)[None, :]
    valid = ((jnp.abs(q_pos[:, :, None] - k_pos[:, None, :]) <= WINDOW)
             & (k_pos[:, None, :] >= 0) & (k_pos[:, None, :] < length))
    s_win = jnp.einsum('bnqgrd,bnkgd->bngrqk', qb, kb).astype(jnp.float32)
    s_win = jnp.where(valid[None, :, None, None], s_win, -jnp.inf)
    s_ctx = jnp.einsum('bnqgrd,bcgd->bngrqc', qb, k_ctx).astype(jnp.float32)
    s_sink = jnp.broadcast_to(sink.astype(jnp.float32).reshape(1, 1, ATTN_KV_HEADS, ATTN_GROUP, 1, 1),
                              s_ctx.shape[:-1] + (1,))
    p = jax.nn.softmax(jnp.concatenate([s_sink, s_ctx, s_win], axis=-1), axis=-1).astype(v.dtype)
    n_ctx = k_ctx.shape[1]
    out = (jnp.einsum('bngrqc,bcgd->bnqgrd', p[..., 1:1 + n_ctx], v_ctx)
           + jnp.einsum('bngrqk,bnkgd->bnqgrd', p[..., 1 + n_ctx:], vb))
    return out.reshape(b_, length, ATTN_HEADS * ATTN_HEAD_DIM)


def context_attention(q, k, v, sink):
    b_, n_ctx = q.shape[:2]
    qc = q.reshape(b_, n_ctx, ATTN_KV_HEADS, ATTN_GROUP, ATTN_HEAD_DIM)
    s = jnp.einsum('bqgrd,bkgd->bgrqk', qc, k).astype(jnp.float32)
    s_sink = jnp.broadcast_to(sink.astype(jnp.float32).reshape(1, ATTN_KV_HEADS, ATTN_GROUP, 1, 1), s.shape[:-1] + (1,))
    p = jax.nn.softmax(jnp.concatenate([s_sink, s], axis=-1), axis=-1).astype(v.dtype)
    out = jnp.einsum('bgrqk,bkgd->bqgrd', p[..., 1:], v)
    return out.reshape(b_, n_ctx, ATTN_HEADS * ATTN_HEAD_DIM)


def branch_merge(o_hg, g, o_attn, gate_a, gate_b, hg_gain, w_a, w_b, w_o):
    a = _rms(o_hg).reshape(g.shape) * hg_gain * jax.nn.silu(g)
    y = jax.nn.sigmoid(gate_a) * (a @ w_a) + jax.nn.sigmoid(gate_b) * (o_attn @ w_b)
    return y @ w_o


def token_mixer(h_ctx, h_lat, cos, sin, w_in, lb_fwd, lb_bwd, hg_gain, sink, w_a, w_b, w_o, need_ctx):
    splits = [sum(IN_WIDTHS[:i + 1]) for i in range(len(IN_WIDTHS) - 1)]
    pc = jnp.split(h_ctx @ w_in, splits, axis=-1)
    pl = jnp.split(h_lat @ w_in, splits, axis=-1)

    def hgrn_inputs(p):
        q = heads(p[0], HG_HEADS) * HG_HEAD_DIM ** -0.5
        v = heads(p[3], HG_HEADS)
        lf_f, k_f = forget_gate(p[1], lb_fwd)
        lf_b, k_b = forget_gate(p[2], lb_bwd)
        return (q, k_f, v, lf_f), (q, k_b, v, lf_b)

    c_fwd, c_bwd = hgrn_inputs(pc)
    l_fwd, l_bwd = hgrn_inputs(pl)
    oc_f, ol_f = hgrn2_direction(c_fwd, l_fwd, reverse=False)
    oc_b, ol_b = hgrn2_direction(c_bwd, l_bwd, reverse=True)

    scale = ATTN_HEAD_DIM ** -0.5
    k_ctx, v_ctx = heads(pc[6], ATTN_KV_HEADS), heads(pc[7], ATTN_KV_HEADS)
    q_lat = apply_rope(heads(pl[5], ATTN_HEADS), cos, sin) * scale
    k_lat = apply_rope(heads(pl[6], ATTN_KV_HEADS), cos, sin)
    a_lat = window_attention(q_lat, k_lat, heads(pl[7], ATTN_KV_HEADS), k_ctx, v_ctx, sink)
    y_lat = branch_merge((ol_f + ol_b).astype(h_lat.dtype), pl[4], a_lat, pl[8], pl[9], hg_gain, w_a, w_b, w_o)
    if not need_ctx:
        return y_lat, None
    a_ctx = context_attention(heads(pc[5], ATTN_HEADS) * scale, k_ctx, v_ctx, sink)
    y_ctx = branch_merge((oc_f + oc_b).astype(h_ctx.dtype), pc[4], a_ctx, pc[8], pc[9], hg_gain, w_a, w_b, w_o)
    return y_lat, y_ctx


def setup_inputs(seed: int = 0) -> dict:
    key = jax.random.key(seed)
    ks = jax.random.split(key, 24)
    n_dense = (DEPTH + 1) // 2
    n_moe = DEPTH // 2

    def nrm(k, shape, scale=1.0):
        return jax.random.normal(k, shape, jnp.float32) * scale

    return {
        'x': nrm(ks[0], (BATCH, SEQ, D_MODEL)),
        'c': nrm(ks[1], (BATCH, D_MODEL)),
        'ctx': nrm(ks[2], (BATCH, CTX_LEN, D_MODEL)),
        'c_ctx': nrm(ks[3], (D_MODEL,)),
        'w_mod': nrm(ks[4], (DEPTH, D_MODEL, 6 * D_MODEL), 0.5 * D_MODEL ** -0.5),
        'b_mod': nrm(ks[5], (DEPTH, 6 * D_MODEL), 0.02),
        'norm_mix': 1.0 + nrm(ks[6], (DEPTH, D_MODEL), 0.02),
        'norm_ffn': 1.0 + nrm(ks[7], (DEPTH, D_MODEL), 0.02),
        'w_in': nrm(ks[8], (DEPTH, D_MODEL, N_IN), D_MODEL ** -0.5),
        'hg_lb_fwd': nrm(ks[9], (DEPTH, MIX_WIDTH)),
        'hg_lb_bwd': nrm(ks[10], (DEPTH, MIX_WIDTH)),
        'hg_norm': 1.0 + nrm(ks[11], (DEPTH, MIX_WIDTH), 0.02),
        'attn_sink': nrm(ks[12], (DEPTH, ATTN_HEADS), 0.5),
        'w_branch_a': nrm(ks[13], (DEPTH, MIX_WIDTH, D_MODEL), MIX_WIDTH ** -0.5),
        'w_branch_b': nrm(ks[14], (DEPTH, MIX_WIDTH, D_MODEL), MIX_WIDTH ** -0.5),
        'w_out': nrm(ks[15], (DEPTH, D_MODEL, D_MODEL), D_MODEL ** -0.5),
        'ffn_w_gate': nrm(ks[16], (n_dense, D_MODEL, D_FF), D_MODEL ** -0.5),
        'ffn_w_up': nrm(ks[17], (n_dense, D_MODEL, D_FF), D_MODEL ** -0.5),
        'ffn_w_down': nrm(ks[18], (n_dense, D_FF, D_MODEL), D_FF ** -0.5),
        'moe_router': nrm(ks[19], (n_moe, D_MODEL, N_EXPERTS), D_MODEL ** -0.5),
        'moe_w_gate': nrm(ks[20], (n_moe, N_EXPERTS, D_MODEL, D_EXPERT), D_MODEL ** -0.5),
        'moe_w_up': nrm(ks[21], (n_moe, N_EXPERTS, D_MODEL, D_EXPERT), D_MODEL ** -0.5),
        'moe_w_down': nrm(ks[22], (n_moe, N_EXPERTS, D_EXPERT, D_MODEL), D_EXPERT ** -0.5),
        'final_norm': 1.0 + nrm(ks[23], (D_MODEL,), 0.02),
    }


def reference(x, c, ctx, c_ctx, w_mod, b_mod, norm_mix, norm_ffn, w_in, hg_lb_fwd, hg_lb_bwd, hg_norm,
              attn_sink, w_branch_a, w_branch_b, w_out, ffn_w_gate, ffn_w_up, ffn_w_down,
              moe_router, moe_w_gate, moe_w_up, moe_w_down, final_norm):
    cos, sin = rope_tables(x.shape[1])
    c_act, c_ctx_act = jax.nn.silu(c), jax.nn.silu(c_ctx)
    for layer in range(DEPTH):
        need_ctx = layer < DEPTH - 1
        mod_lat = jnp.split((c_act @ w_mod[layer] + b_mod[layer])[:, None, :], 6, axis=-1)
        mod_ctx = jnp.split(c_ctx_act @ w_mod[layer] + b_mod[layer], 6, axis=-1)
        h_lat = modulate(rmsnorm(x, norm_mix[layer]), mod_lat[0], mod_lat[1])
        h_ctx = modulate(rmsnorm(ctx, norm_mix[layer]), mod_ctx[0], mod_ctx[1])
        y_lat, y_ctx = token_mixer(h_ctx, h_lat, cos, sin, w_in[layer],
                                   lower_bound(hg_lb_fwd, layer), lower_bound(hg_lb_bwd, layer),
                                   hg_norm[layer], attn_sink[layer], w_branch_a[layer], w_branch_b[layer],
                                   w_out[layer], need_ctx)
        x = x + mod_lat[2] * y_lat
        h_lat = modulate(rmsnorm(x, norm_ffn[layer]), mod_lat[3], mod_lat[4])
        x = x + mod_lat[5] * channel_mixer(h_lat, layer, ffn_w_gate, ffn_w_up, ffn_w_down,
                                           moe_router, moe_w_gate, moe_w_up, moe_w_down)
        if need_ctx:
            ctx = ctx + mod_ctx[2] * y_ctx
            h_ctx = modulate(rmsnorm(ctx, norm_ffn[layer]), mod_ctx[3], mod_ctx[4])
            ctx = ctx + mod_ctx[5] * channel_mixer(h_ctx, layer, ffn_w_gate, ffn_w_up, ffn_w_down,
                                                   moe_router, moe_w_gate, moe_w_up, moe_w_down)
    return rmsnorm(x, final_norm)
```

```python
import functools

import jax
import jax.numpy as jnp
from jax import lax
from jax.experimental import pallas as pl
from jax.experimental.pallas import tpu as pltpu

HG_HEAD_DIM = 128
ATTN_HEAD_DIM = 64
ATTN_GROUP = 8
ATTN_BLOCK = 128
GRID_W = 64
ROPE_THETA = 10000.0
EPS = 1e-6
TOP_K = 2
LANES = 128
HG_CHUNK = 128
HG_SUB = 16
VMEM_LIMIT = 56 * 1024 * 1024

F32 = jnp.float32
BF16 = jnp.bfloat16


def _pick_tile(n, cap, mult=16):
    best = None
    for t in range(mult, min(cap, n) + 1, mult):
        if n % t == 0:
            best = t
    assert best is not None, (n, cap)
    return best


def _params(sem, vmem=VMEM_LIMIT, **kw):
    return pltpu.CompilerParams(dimension_semantics=sem, vmem_limit_bytes=vmem, **kw)


def _dot(a, b):
    return jnp.dot(a, b, preferred_element_type=F32)


def _dot_nt(a, b):
    return lax.dot_general(a, b, (((1,), (1,)), ((), ())), preferred_element_type=F32)


def _dot_tn(a, b):
    return lax.dot_general(a, b, (((0,), (0,)), ((), ())), preferred_element_type=F32)


def _norm_mod(x, gain, shift, scale):
    ms = jnp.mean(x * x, axis=-1, keepdims=True)
    return x * lax.rsqrt(ms + EPS) * gain * (1.0 + scale) + shift


def _is_ctx_rows(row0, n, seq):
    rows = row0 + lax.broadcasted_iota(jnp.int32, (n, 1), 0)
    return rows >= seq


def _mod_row(modl_ref, modc_ref, k, is_ctx):
    return jnp.where(is_ctx, modc_ref[0, k:k + 1, :], modl_ref[0, k:k + 1, :])


def _for_row_chunks(tm, body):
    rc = _pick_tile(tm, 256)

    def step(k, carry):
        body(pl.multiple_of(k * rc, rc), rc)
        return carry

    lax.fori_loop(0, tm // rc, step, 0)


def _norm_mod_rows(x_ref, h_ref, g_ref, modl_ref, modc_ref, k_shift, tile_row0, tm, seq):
    def body(r0, rc):
        rows = pl.ds(r0, rc)
        is_ctx = _is_ctx_rows(tile_row0 + r0, rc, seq)
        h = _norm_mod(x_ref[0, rows, :], g_ref[...], _mod_row(modl_ref, modc_ref, k_shift, is_ctx),
                      _mod_row(modl_ref, modc_ref, k_shift + 1, is_ctx))
        h_ref[rows, :] = h.astype(h_ref.dtype)

    _for_row_chunks(tm, body)


def _mod_kernel(c_ref, w_ref, b_ref, o_ref):
    act = jax.nn.silu(c_ref[...])
    o_ref[0] = jnp.dot(act, w_ref[0], preferred_element_type=F32,
                       precision=lax.Precision.HIGHEST) + b_ref[0]


def _modulation(cvec, w_mod, b_mod):
    depth, d, n6 = w_mod.shape
    mb = cvec.shape[0]
    tn = _pick_tile(n6, 1024, LANES)
    return pl.pallas_call(
        _mod_kernel,
        grid=(depth, n6 // tn),
        in_specs=[pl.BlockSpec((mb, d), lambda l, j: (0, 0)),
                  pl.BlockSpec((1, d, tn), lambda l, j: (l, 0, j)),
                  pl.BlockSpec((1, 1, tn), lambda l, j: (l, 0, j))],
        out_specs=pl.BlockSpec((1, mb, tn), lambda l, j: (l, 0, j)),
        out_shape=jax.ShapeDtypeStruct((depth, mb, n6), F32),
        compiler_params=_params(("parallel", "arbitrary")),
        name="modulation",
    )(cvec, w_mod, b_mod.reshape(depth, 1, n6))


def _in_proj_kernel(x_ref, modl_ref, modc_ref, g_ref, w_ref, o_ref, h_ref, *, tm, seq):
    tile_row0 = pl.program_id(1) * tm

    @pl.when(pl.program_id(2) == 0)
    def _():
        _norm_mod_rows(x_ref, h_ref, g_ref, modl_ref, modc_ref, 0, tile_row0, tm, seq)

    o_ref[0] = _dot(h_ref[...], w_ref[...])


def _in_proj(x, mods, gain, w, *, seq, tm, tn):
    b, r, d = x.shape
    npad = w.shape[1]
    nb = mods.shape[0] - 1
    return pl.pallas_call(
        functools.partial(_in_proj_kernel, tm=tm, seq=seq),
        grid=(b, r // tm, npad // tn),
        in_specs=[pl.BlockSpec((1, tm, d), lambda i, t, j: (i, t, 0)),
                  pl.BlockSpec((1, 6, d), lambda i, t, j: (i, 0, 0)),
                  pl.BlockSpec((1, 6, d), lambda i, t, j: (nb, 0, 0)),
                  pl.BlockSpec((1, d), lambda i, t, j: (0, 0)),
                  pl.BlockSpec((d, tn), lambda i, t, j: (0, j))],
        out_specs=pl.BlockSpec((1, tm, tn), lambda i, t, j: (i, t, j)),
        out_shape=jax.ShapeDtypeStruct((b, r, npad), F32),
        scratch_shapes=[pltpu.VMEM((tm, d), BF16)],
        compiler_params=_params(("parallel", "parallel", "arbitrary")),
        name="in_proj",
    )(x, mods, mods, gain, w)


def _hgrn_kernel(*refs, rev, chunk, sub, nheads, final):
    if final:
        q_ref, f_ref, v_ref, lb_ref, ob_ref, g_ref, gain_ref, o_ref, st_ref = refs
    else:
        q_ref, f_ref, v_ref, lb_ref, o_ref, st_ref = refs
    c = chunk
    hd = HG_HEAD_DIM

    @pl.when(pl.program_id(1) == 0)
    def _():
        st_ref[...] = jnp.zeros_like(st_ref)

    lb = lb_ref[...]
    f = lb + (1.0 - lb) * jax.nn.sigmoid(f_ref[0])
    logf = jnp.log(f)
    rowi = lax.broadcasted_iota(jnp.int32, (c, c), 0)
    coli = lax.broadcasted_iota(jnp.int32, (c, c), 1)
    tri = ((coli >= rowi) if rev else (coli <= rowi)).astype(BF16)
    g_hi = logf.astype(BF16)
    r1 = logf - g_hi.astype(F32)
    g_mid = r1.astype(BF16)
    g_lo = (r1 - g_mid.astype(F32)).astype(BF16)
    bcum = _dot(tri, g_hi) + _dot(tri, g_mid) + _dot(tri, g_lo)

    rows = lax.broadcasted_iota(jnp.int32, (c, 1), 0)
    scale = HG_HEAD_DIM ** -0.5
    last = 0 if rev else c - 1

    for h in range(nheads):
        sl = slice(h * hd, (h + 1) * hd)
        qh = q_ref[0, :, sl] * scale
        kh = 1.0 - f[:, sl]
        vh = v_ref[0, :, sl]
        bh = bcum[:, sl]
        btot = bh[last:last + 1, :]
        st = st_ref[h]
        o = _dot_nt((qh * jnp.exp(bh)).astype(BF16), st.astype(BF16))

        scores = jnp.zeros((c, c), F32)
        half = c // 2
        while half >= sub:
            blk = 2 * half
            ref0 = half if rev else half - 1
            bref = bh[ref0:ref0 + 1, :]
            for k in range(1, c // blk):
                rk = k * blk + ref0
                bref = jnp.where(rows >= k * blk, bh[rk:rk + 1, :], bref)
            second = (rows // half) % 2 == 1
            qside = jnp.logical_not(second) if rev else second
            e = jnp.exp(jnp.where(qside, bh - bref, bref - bh))
            qt = jnp.where(qside, qh * e, 0.0).astype(BF16)
            kt = jnp.where(qside, 0.0, kh * e).astype(BF16)
            same = (rowi // blk) == (coli // blk)
            scores = scores + jnp.where(same, _dot_nt(qt, kt), 0.0)
            half //= 2
        o = o + _dot(scores.astype(BF16), vh.astype(BF16))

        o = o + jnp.sum(qh * kh, axis=-1, keepdims=True) * vh
        pos = rows % sub
        for delta in range(1, sub):
            shift = (c - delta) if rev else delta
            bs = pltpu.roll(bh, shift, 0)
            ks = pltpu.roll(kh, shift, 0)
            vs = pltpu.roll(vh, shift, 0)
            ok = (pos + delta < sub) if rev else (pos >= delta)
            e = jnp.exp(jnp.where(ok, bh - bs, -jnp.inf))
            a = jnp.sum(qh * ks * e, axis=-1, keepdims=True)
            o = o + a * vs

        khat = (kh * jnp.exp(btot - bh)).astype(BF16)
        st_ref[h] = st * jnp.exp(btot) + _dot_tn(vh.astype(BF16), khat)

        if final:
            osum = o + ob_ref[0, :, sl]
            ms = jnp.mean(osum * osum, axis=-1, keepdims=True)
            a_out = osum * lax.rsqrt(ms + EPS) * gain_ref[:, sl] * jax.nn.silu(g_ref[0, :, sl])
            o_ref[0, :, sl] = a_out.astype(o_ref.dtype)
        else:
            o_ref[0, :, sl] = o


def _hgrn(p, lb, *, seq, ctx_len, mix, rev, fcol, extra=None):
    b, r, _ = p.shape
    c = HG_CHUNK
    nl, nc = seq // c, ctx_len // c
    nheads = mix // HG_HEAD_DIM

    if rev:
        def order(n):
            return nl + nc - 1 - n
    else:
        def order(n):
            return jnp.where(n < nc, nl + n, n - nc)

    def pspec(col):
        return pl.BlockSpec((1, c, mix), lambda i, n: (i, order(n), col))

    in_specs = [pspec(0), pspec(fcol), pspec(3), pl.BlockSpec((1, mix), lambda i, n: (0, 0))]
    args = [p, p, p, lb]
    final = extra is not None
    if final:
        o_other, gain = extra
        in_specs += [pl.BlockSpec((1, c, mix), lambda i, n: (i, order(n), 0)), pspec(4),
                     pl.BlockSpec((1, mix), lambda i, n: (0, 0))]
        args += [o_other, p, gain]
    return pl.pallas_call(
        functools.partial(_hgrn_kernel, rev=rev, chunk=c, sub=HG_SUB, nheads=nheads, final=final),
        grid=(b, nl + nc),
        in_specs=in_specs,
        out_specs=pl.BlockSpec((1, c, mix), lambda i, n: (i, order(n), 0)),
        out_shape=jax.ShapeDtypeStruct((b, r, mix), BF16 if final else F32),
        scratch_shapes=[pltpu.VMEM((nheads, HG_HEAD_DIM, HG_HEAD_DIM), F32)],
        compiler_params=_params(("parallel", "arbitrary")),
        name="hgrn_bwd" if rev else "hgrn_fwd",
    )(*args)


def _rope(x, cos, sin):
    lane = lax.broadcasted_iota(jnp.int32, x.shape, 1)
    first = (lane % 32) < 16
    swapped = jnp.where(first, pltpu.roll(x, LANES - 16, 1), pltpu.roll(x, 16, 1))
    return x * cos + swapped * sin


def _attn_kernel(*refs, windowed, nblk, kvh, ctx_len):
    if windowed:
        (q_ref, kp_ref, kc_ref, kn_ref, vp_ref, vc_ref, vn_ref, kx_ref, vx_ref,
         cq_ref, sq_ref, cp_ref, sp_ref, cn_ref, sn_ref, sink_ref, o_ref) = refs
    else:
        q_ref, kx_ref, vx_ref, sink_ref, o_ref = refs
    blk = ATTN_BLOCK
    hd = ATTN_HEAD_DIM
    gw = ATTN_GROUP * hd
    n = pl.program_id(1)
    scale = ATTN_HEAD_DIM ** -0.5

    if windowed:
        kwin = [_rope(kp_ref[0], cp_ref[...], sp_ref[...]),
                _rope(kc_ref[0], cq_ref[...], sq_ref[...]),
                _rope(kn_ref[0], cn_ref[...], sn_ref[...])]
        vwin = [vp_ref[0], vc_ref[0], vn_ref[0]]
        nkeys = ctx_len + 3 * blk
        qi = lax.broadcasted_iota(jnp.int32, (blk, nkeys), 0)
        kj = lax.broadcasted_iota(jnp.int32, (blk, nkeys), 1) - ctx_len
        valid = (kj < 0) | ((kj >= qi) & (kj <= qi + 2 * blk)
                            & ((kj >= blk) | (n > 0)) & ((kj < 2 * blk) | (n < nblk - 1)))

    for g in range(kvh):
        ks = slice(g * hd, (g + 1) * hd)
        k_parts = [kx_ref[0, :, ks]]
        v_parts = [vx_ref[0, :, ks]]
        if windowed:
            k_parts += [kw[:, ks] for kw in kwin]
            v_parts += [vw[:, ks] for vw in vwin]
        k_all = jnp.concatenate(k_parts, axis=0).astype(BF16)
        v_all = jnp.concatenate(v_parts, axis=0).astype(BF16)

        q_heads = []
        for c4 in range(gw // LANES):
            col = g * gw + c4 * LANES
            qc = q_ref[0, :, col:col + LANES]
            if windowed:
                qc = _rope(qc, cq_ref[...], sq_ref[...])
            qc = qc * scale
            q_heads += [qc[:, :hd], qc[:, hd:]]
        q_stack = jnp.concatenate(q_heads, axis=0).astype(BF16)
        s_all = _dot_nt(q_stack, k_all)

        p_list, inv_list = [], []
        for r in range(ATTN_GROUP):
            s = s_all[r * blk:(r + 1) * blk, :]
            if windowed:
                s = jnp.where(valid, s, -jnp.inf)
            sink = sink_ref[g * ATTN_GROUP + r]
            m = jnp.maximum(jnp.max(s, axis=-1, keepdims=True), sink)
            p = jnp.exp(s - m)
            denom = jnp.sum(p, axis=-1, keepdims=True) + jnp.exp(sink - m)
            p_list.append(p.astype(BF16))
            inv_list.append(1.0 / denom)
        o_all = _dot(jnp.concatenate(p_list, axis=0), v_all)
        outs = [o_all[r * blk:(r + 1) * blk, :] * inv_list[r] for r in range(ATTN_GROUP)]
        o_ref[0, :, g * gw:(g + 1) * gw] = jnp.concatenate(outs, axis=1).astype(o_ref.dtype)


def _attention(p, cos, sin, sink, *, seq, ctx_len, mix, kvw):
    b, r, _ = p.shape
    blk = ATTN_BLOCK
    nblk = seq // blk
    kvh = kvw // ATTN_HEAD_DIM
    kcol = 6 * mix // kvw
    ctx_blk = seq // ctx_len

    def kv(col, off):
        return pl.BlockSpec((1, blk, kvw),
                            lambda i, n: (i, jnp.clip(n + off, 0, nblk - 1), col))

    def tab(off):
        return pl.BlockSpec((blk, LANES), lambda i, n: (jnp.clip(n + off, 0, nblk - 1), 0))

    smem = pl.BlockSpec(memory_space=pltpu.SMEM)
    ctx_k = pl.BlockSpec((1, ctx_len, kvw), lambda i, n: (i, ctx_blk, kcol))
    ctx_v = pl.BlockSpec((1, ctx_len, kvw), lambda i, n: (i, ctx_blk, kcol + 1))
    out_shape = jax.ShapeDtypeStruct((b, r, mix), BF16)

    lat = pl.pallas_call(
        functools.partial(_attn_window_kernel, windowed=True, nblk=nblk, kvh=kvh, ctx_len=ctx_len),
        grid=(b, r // blk),
        in_specs=[pl.BlockSpec((1, blk, mix), lambda i, n: (i, jnp.minimum(n, nblk - 1), 5)),
                  kv(kcol, -1), kv(kcol, 0), kv(kcol, 1),
                  kv(kcol + 1, -1), kv(kcol + 1, 0), kv(kcol + 1, 1),
                  ctx_k, ctx_v,
                  tab(0), tab(0), tab(-1), tab(-1), tab(1), tab(1), smem],
        out_specs=pl.BlockSpec((1, blk, mix), lambda i, n: (i, n, 0)),
        out_shape=out_shape,
        compiler_params=_params(("parallel", "arbitrary")),
        name="attn_window",
    )(p, p, p, p, p, p, p, p, p, cos, sin, cos, sin, cos, sin, sink)
    return lat, (ctx_k, ctx_v, smem, out_shape, kvh)


def _context_attention(p, attn_lat, sink, aux, *, seq, ctx_len, mix):
    ctx_k, ctx_v, smem, out_shape, kvh = aux
    b = p.shape[0]
    blk = ATTN_BLOCK
    base = seq // blk
    return pl.pallas_call(
        functools.partial(_attn_ctx_kernel, windowed=False, nblk=0, kvh=kvh, ctx_len=ctx_len),
        grid=(b, ctx_len // blk),
        in_specs=[pl.BlockSpec((1, blk, mix), lambda i, n: (i, base + n, 5)),
                  ctx_k, ctx_v, smem,
                  pl.BlockSpec(memory_space=pl.ANY)],
        out_specs=pl.BlockSpec((1, blk, mix), lambda i, n: (i, base + n, 0)),
        out_shape=out_shape,
        input_output_aliases={4: 0},
        compiler_params=_params(("parallel", "arbitrary")),
        name="attn_context",
    )(p, p, p, sink, attn_lat)


def _attn_ctx_kernel(q_ref, kx_ref, vx_ref, sink_ref, prev_ref, o_ref, **kw):
    del prev_ref
    _attn_kernel(q_ref, kx_ref, vx_ref, sink_ref, o_ref, **kw)


def _attn_window_kernel(*refs, nblk, **kw):
    n = pl.program_id(1)
    o_ref = refs[-1]

    @pl.when(n < nblk)
    def _():
        _attn_kernel(*refs, nblk=nblk, **kw)

    @pl.when(n >= nblk)
    def _():
        o_ref[...] = jnp.zeros_like(o_ref)


def _merge_kernel(a_ref, t_ref, ga_ref, gb_ref, wa_ref, wb_ref, o_ref):
    ya = _dot(a_ref[0], wa_ref[...])
    yb = _dot(t_ref[0], wb_ref[...])
    y = jax.nn.sigmoid(ga_ref[0]) * ya + jax.nn.sigmoid(gb_ref[0]) * yb
    o_ref[0] = y.astype(o_ref.dtype)


def _merge(a, attn, p, wa, wb, *, tm, ntiles, tn, ga_col, gb_col):
    b, r, mix = a.shape
    d = wa.shape[1]
    return pl.pallas_call(
        _merge_kernel,
        grid=(b, ntiles, d // tn),
        in_specs=[pl.BlockSpec((1, tm, mix), lambda i, t, j: (i, t, 0)),
                  pl.BlockSpec((1, tm, mix), lambda i, t, j: (i, t, 0)),
                  pl.BlockSpec((1, tm, tn), lambda i, t, j: (i, t, ga_col + j)),
                  pl.BlockSpec((1, tm, tn), lambda i, t, j: (i, t, gb_col + j)),
                  pl.BlockSpec((mix, tn), lambda i, t, j: (0, j)),
                  pl.BlockSpec((mix, tn), lambda i, t, j: (0, j))],
        out_specs=pl.BlockSpec((1, tm, tn), lambda i, t, j: (i, t, j)),
        out_shape=jax.ShapeDtypeStruct((b, ntiles * tm, d), BF16),
        compiler_params=_params(("parallel", "parallel", "arbitrary")),
        name="branch_merge",
    )(a, attn, p, p, wa, wb)


def _out_proj_kernel(y_ref, w_ref, x_ref, modl_ref, modc_ref, o_ref, *, tm, seq):
    is_ctx = _is_ctx_rows(pl.program_id(1) * tm, tm, seq)
    gate = jnp.where(is_ctx, modc_ref[0, 2:3, :], modl_ref[0, 2:3, :])
    o_ref[0] = x_ref[0] + gate * _dot(y_ref[0], w_ref[...])


def _out_proj(y, w, x, mods, *, seq, tm, ntiles, tn):
    b, r, d = x.shape
    nb = mods.shape[0] - 1
    return pl.pallas_call(
        functools.partial(_out_proj_kernel, tm=tm, seq=seq),
        grid=(b, ntiles, d // tn),
        in_specs=[pl.BlockSpec((1, tm, d), lambda i, t, j: (i, t, 0)),
                  pl.BlockSpec((d, tn), lambda i, t, j: (0, j)),
                  pl.BlockSpec((1, tm, tn), lambda i, t, j: (i, t, j)),
                  pl.BlockSpec((1, 6, tn), lambda i, t, j: (i, 0, j)),
                  pl.BlockSpec((1, 6, tn), lambda i, t, j: (nb, 0, j))],
        out_specs=pl.BlockSpec((1, tm, tn), lambda i, t, j: (i, t, j)),
        out_shape=jax.ShapeDtypeStruct((b, ntiles * tm, d), F32),
        compiler_params=_params(("parallel", "parallel", "arbitrary")),
        name="out_proj",
    )(y, w, x, mods, mods)


def _ffn_kernel(x_ref, modl_ref, modc_ref, g_ref, wg_ref, wu_ref, wd_ref, o_ref, h_ref, acc_ref,
                *, tm, seq):
    j = pl.program_id(2)
    last = pl.num_programs(2) - 1
    tile_row0 = pl.program_id(1) * tm

    @pl.when(j == 0)
    def _():
        _norm_mod_rows(x_ref, h_ref, g_ref, modl_ref, modc_ref, 3, tile_row0, tm, seq)
        acc_ref[...] = jnp.zeros_like(acc_ref)

    h = h_ref[...]
    act = jax.nn.silu(_dot(h, wg_ref[...])) * _dot(h, wu_ref[...])
    acc_ref[...] += _dot(act.astype(BF16), wd_ref[...])

    @pl.when(j == last)
    def _():
        def body(r0, rc):
            rows = pl.ds(r0, rc)
            is_ctx = _is_ctx_rows(tile_row0 + r0, rc, seq)
            o_ref[0, rows, :] = (x_ref[0, rows, :]
                                 + _mod_row(modl_ref, modc_ref, 5, is_ctx) * acc_ref[rows, :])

        _for_row_chunks(tm, body)


def _ffn(x, mods, gain, wg, wu, wd, *, seq, tm, tf):
    b, r, d = x.shape
    f = wg.shape[1]
    nb = mods.shape[0] - 1
    return pl.pallas_call(
        functools.partial(_ffn_kernel, tm=tm, seq=seq),
        grid=(b, r // tm, f // tf),
        in_specs=[pl.BlockSpec((1, tm, d), lambda i, t, j: (i, t, 0)),
                  pl.BlockSpec((1, 6, d), lambda i, t, j: (i, 0, 0)),
                  pl.BlockSpec((1, 6, d), lambda i, t, j: (nb, 0, 0)),
                  pl.BlockSpec((1, d), lambda i, t, j: (0, 0)),
                  pl.BlockSpec((d, tf), lambda i, t, j: (0, j)),
                  pl.BlockSpec((d, tf), lambda i, t, j: (0, j)),
                  pl.BlockSpec((tf, d), lambda i, t, j: (j, 0))],
        out_specs=pl.BlockSpec((1, tm, d), lambda i, t, j: (i, t, 0)),
        out_shape=jax.ShapeDtypeStruct((b, r, d), F32),
        scratch_shapes=[pltpu.VMEM((tm, d), BF16), pltpu.VMEM((tm, d), F32)],
        compiler_params=_params(("parallel", "parallel", "arbitrary")),
        name="ffn_dense",
    )(x, mods, mods, gain, wg, wu, wd)


def _route_kernel(x_ref, modl_ref, g_ref, wr_ref, h_ref, r_ref, *, n_experts, tm):
    def body(r0, rc):
        rows = pl.ds(r0, rc)
        h = _norm_mod(x_ref[0, rows, :], g_ref[...], modl_ref[0, 3:4, :], modl_ref[0, 4:5, :])
        h_ref[rows, :] = h
        logits = jnp.dot(h, wr_ref[...], preferred_element_type=F32, precision=lax.Precision.HIGHEST)
        lane = lax.broadcasted_iota(jnp.int32, logits.shape, 1)
        lg = jnp.where(lane < n_experts, logits, -jnp.inf)
        m1 = jnp.max(lg, axis=-1, keepdims=True)
        i1 = jnp.min(jnp.where(lg == m1, lane, LANES), axis=-1, keepdims=True)
        lg2 = jnp.where(lane == i1, -jnp.inf, lg)
        m2 = jnp.max(lg2, axis=-1, keepdims=True)
        i2 = jnp.min(jnp.where(lg2 == m2, lane, LANES), axis=-1, keepdims=True)
        e2 = jnp.exp(m2 - m1)
        w1 = 1.0 / (1.0 + e2)
        w2 = e2 / (1.0 + e2)
        r_ref[rows, :] = jnp.where(lane == 0, i1.astype(F32),
                                   jnp.where(lane == 1, i2.astype(F32),
                                             jnp.where(lane == 2, w1, jnp.where(lane == 3, w2, 0.0))))

    _for_row_chunks(tm, body)


def _route(x, mods, gain, wr, *, seq, tm, n_experts):
    b, r, d = x.shape
    nt = seq // tm
    return pl.pallas_call(
        functools.partial(_route_kernel, n_experts=n_experts, tm=tm),
        grid=(b, nt),
        in_specs=[pl.BlockSpec((1, tm, d), lambda i, t: (i, t, 0)),
                  pl.BlockSpec((1, 6, d), lambda i, t: (i, 0, 0)),
                  pl.BlockSpec((1, d), lambda i, t: (0, 0)),
                  pl.BlockSpec((d, LANES), lambda i, t: (0, 0))],
        out_specs=[pl.BlockSpec((tm, d), lambda i, t: (i * nt + t, 0)),
                   pl.BlockSpec((tm, LANES), lambda i, t: (i * nt + t, 0))],
        out_shape=[jax.ShapeDtypeStruct((b * seq, d), F32),
                   jax.ShapeDtypeStruct((b * seq, LANES), F32)],
        compiler_params=_params(("parallel", "parallel")),
        name="moe_route",
    )(x, mods, gain, wr)


def _gather_kernel(idx_ref, src_ref, o_ref, sem, *, gm):
    def issue(r, carry):
        pltpu.make_async_copy(src_ref.at[pl.ds(idx_ref[0, 0, r], 1), :],
                              o_ref.at[pl.ds(r, 1), :], sem).start()
        return carry

    lax.fori_loop(0, gm, issue, 0)

    def drain(r, carry):
        pltpu.make_async_copy(src_ref.at[pl.ds(0, 1), :], o_ref.at[pl.ds(r, 1), :], sem).wait()
        return carry

    lax.fori_loop(0, gm, drain, 0)


def _gather_rows(src, idx, *, gm):
    n = idx.shape[0]
    d = src.shape[1]
    return pl.pallas_call(
        functools.partial(_gather_kernel, gm=gm),
        grid=(n // gm,),
        in_specs=[pl.BlockSpec((1, 1, gm), lambda i: (i, 0, 0), memory_space=pltpu.SMEM),
                  pl.BlockSpec(memory_space=pl.ANY)],
        out_specs=pl.BlockSpec((gm, d), lambda i: (i, 0)),
        out_shape=jax.ShapeDtypeStruct((n, d), src.dtype),
        scratch_shapes=[pltpu.SemaphoreType.DMA(())],
        compiler_params=_params(("arbitrary",)),
        name="moe_gather",
    )(idx.reshape(n // gm, 1, gm), src)


def _moe_ffn_kernel(te_ref, nu_ref, x_ref, wg_ref, wu_ref, wd_ref, o_ref, h_ref, acc_ref):
    i, j = pl.program_id(0), pl.program_id(1)
    last = pl.num_programs(1) - 1
    used = i < nu_ref[0]

    @pl.when(j == 0)
    def _():
        h_ref[...] = x_ref[...].astype(BF16)
        acc_ref[...] = jnp.zeros_like(acc_ref)

    @pl.when(used)
    def _():
        h = h_ref[...]
        act = jax.nn.silu(_dot(h, wg_ref[0])) * _dot(h, wu_ref[0])
        acc_ref[...] += _dot(act.astype(BF16), wd_ref[0])

    @pl.when(j == last)
    def _():
        o_ref[...] = acc_ref[...]


def _moe_ffn(xs, tile_expert, n_used, wg, wu, wd, *, tm, tf):
    ns, d = xs.shape
    f = wg.shape[2]

    def row(i, j, te, nu):
        return (jnp.minimum(i, nu[0] - 1), 0)

    def w_up(i, j, te, nu):
        return (te[i], 0, jnp.where(i < nu[0], j, 0))

    def w_down(i, j, te, nu):
        return (te[i], jnp.where(i < nu[0], j, 0), 0)

    grid_spec = pltpu.PrefetchScalarGridSpec(
        num_scalar_prefetch=2,
        grid=(ns // tm, f // tf),
        in_specs=[pl.BlockSpec((tm, d), row),
                  pl.BlockSpec((1, d, tf), w_up),
                  pl.BlockSpec((1, d, tf), w_up),
                  pl.BlockSpec((1, tf, d), w_down)],
        out_specs=pl.BlockSpec((tm, d), lambda i, j, te, nu: (i, 0)),
        scratch_shapes=[pltpu.VMEM((tm, d), BF16), pltpu.VMEM((tm, d), F32)])
    return pl.pallas_call(
        _moe_ffn_kernel,
        grid_spec=grid_spec,
        out_shape=jax.ShapeDtypeStruct((ns, d), F32),
        compiler_params=_params(("arbitrary", "arbitrary")),
        name="moe_ffn",
    )(tile_expert, n_used, xs, wg, wu, wd)


def _combine_kernel(s0_ref, s1_ref, x_ref, modl_ref, r_ref, fn_ref, y_ref, o_ref, ya_ref, yb_ref, sem,
                    *, cm):
    def issue(r, carry):
        pltpu.make_async_copy(y_ref.at[pl.ds(s0_ref[0, 0, r], 1), :],
                              ya_ref.at[pl.ds(r, 1), :], sem.at[0]).start()
        pltpu.make_async_copy(y_ref.at[pl.ds(s1_ref[0, 0, r], 1), :],
                              yb_ref.at[pl.ds(r, 1), :], sem.at[1]).start()
        return carry

    lax.fori_loop(0, cm, issue, 0)

    def drain(r, carry):
        pltpu.make_async_copy(y_ref.at[pl.ds(0, 1), :], ya_ref.at[pl.ds(r, 1), :], sem.at[0]).wait()
        pltpu.make_async_copy(y_ref.at[pl.ds(0, 1), :], yb_ref.at[pl.ds(r, 1), :], sem.at[1]).wait()
        return carry

    lax.fori_loop(0, cm, drain, 0)

    route = r_ref[...]
    moe = route[:, 2:3] * ya_ref[...] + route[:, 3:4] * yb_ref[...]
    xn = x_ref[0] + modl_ref[0, 5:6, :] * moe
    ms = jnp.mean(xn * xn, axis=-1, keepdims=True)
    o_ref[0] = xn * lax.rsqrt(ms + EPS) * fn_ref[...]


def _combine(x, mods, route, final_gain, y, slot0, slot1, *, seq, cm):
    b, r, d = x.shape
    nt = seq // cm

    def sidx():
        return pl.BlockSpec((1, 1, cm), lambda i, t: (i * nt + t, 0, 0), memory_space=pltpu.SMEM)

    return pl.pallas_call(
        functools.partial(_combine_kernel, cm=cm),
        grid=(b, nt),
        in_specs=[sidx(), sidx(),
                  pl.BlockSpec((1, cm, d), lambda i, t: (i, t, 0)),
                  pl.BlockSpec((1, 6, d), lambda i, t: (i, 0, 0)),
                  pl.BlockSpec((cm, LANES), lambda i, t: (i * nt + t, 0)),
                  pl.BlockSpec((1, d), lambda i, t: (0, 0)),
                  pl.BlockSpec(memory_space=pl.ANY)],
        out_specs=pl.BlockSpec((1, cm, d), lambda i, t: (i, t, 0)),
        out_shape=jax.ShapeDtypeStruct((b, seq, d), F32),
        scratch_shapes=[pltpu.VMEM((cm, d), F32), pltpu.VMEM((cm, d), F32),
                        pltpu.SemaphoreType.DMA((2,))],
        compiler_params=_params(("arbitrary", "arbitrary")),
        name="moe_combine",
    )(slot0.reshape(b * nt, 1, cm), slot1.reshape(b * nt, 1, cm), x, mods, route, final_gain, y)


def _moe_plan(route, n_experts, tm):
    t = route.shape[0]
    eid = route[:, :TOP_K].astype(jnp.int32).reshape(-1)
    onehot = (eid[:, None] == jnp.arange(n_experts, dtype=jnp.int32)[None, :]).astype(jnp.int32)
    csum = jnp.cumsum(onehot, axis=0)
    rank = jnp.sum((csum - onehot) * onehot, axis=1)
    counts = csum[-1]
    padded = ((counts + tm - 1) // tm) * tm
    ends = jnp.cumsum(padded)
    starts = ends - padded
    slot = starts[eid] + rank
    ns = TOP_K * t + n_experts * tm
    token = jnp.arange(TOP_K * t, dtype=jnp.int32) // TOP_K
    src = jnp.zeros((ns,), jnp.int32).at[slot].set(token)
    tile_start = jnp.arange(ns // tm, dtype=jnp.int32) * tm
    tile_expert = jnp.minimum(jnp.searchsorted(ends, tile_start, side="right"),
                              n_experts - 1).astype(jnp.int32)
    n_used = (ends[-1] // tm).astype(jnp.int32).reshape(1)
    slots = slot.reshape(t, TOP_K)
    return src, tile_expert, n_used, slots[:, 0], slots[:, 1]


def _rope_tables(length):
    rows = length // GRID_W
    row = jnp.repeat(jnp.arange(rows, dtype=F32), GRID_W)
    col = jnp.tile(jnp.arange(GRID_W, dtype=F32), rows)
    n_freq = ATTN_HEAD_DIM // 4
    inv = ROPE_THETA ** (-jnp.arange(n_freq, dtype=F32) / n_freq)
    ar, ac = row[:, None] * inv, col[:, None] * inv
    cos = jnp.concatenate([jnp.cos(ar), jnp.cos(ar), jnp.cos(ac), jnp.cos(ac)], axis=1)
    sin = jnp.concatenate([-jnp.sin(ar), jnp.sin(ar), -jnp.sin(ac), jnp.sin(ac)], axis=1)
    reps = LANES // ATTN_HEAD_DIM
    return jnp.tile(cos, (1, reps)), jnp.tile(sin, (1, reps))


def _lower_bound(raw, layer):
    p = jax.nn.softmax(raw.astype(F32), axis=0)
    return (jnp.cumsum(p, axis=0) - p[0])[layer]


def kernel(x, c, ctx, c_ctx, w_mod, b_mod, norm_mix, norm_ffn, w_in, hg_lb_fwd, hg_lb_bwd, hg_norm,
           attn_sink, w_branch_a, w_branch_b, w_out, ffn_w_gate, ffn_w_up, ffn_w_down,
           moe_router, moe_w_gate, moe_w_up, moe_w_down, final_norm):
    b, seq, d = x.shape
    ctx_len = ctx.shape[1]
    depth = w_mod.shape[0]
    r = seq + ctx_len
    mix = d // 2
    kvw = (mix // ATTN_HEAD_DIM // ATTN_GROUP) * ATTN_HEAD_DIM
    n_experts = moe_router.shape[-1]
    assert depth == 2 and kvw % LANES == 0 and seq % ctx_len == 0
    assert seq % HG_CHUNK == 0 and ctx_len % HG_CHUNK == 0 and seq % GRID_W == 0

    tn = 512
    ga = -(-(6 * mix + 2 * kvw) // tn) * tn
    gb = ga + d
    n_in = 6 * mix + 2 * kvw
    tm_full = _pick_tile(r, 1152)
    tm_lat = _pick_tile(seq, 1024)
    tm_ffn = _pick_tile(r, 576)

    mb = -(-(b + 1) // 8) * 8
    cvec = jnp.zeros((mb, d), F32).at[:b].set(c).at[b].set(c_ctx)
    mods_all = _modulation(cvec, w_mod, b_mod)[:, :b + 1].reshape(depth, b + 1, 6, d)

    cos, sin = _rope_tables(seq)
    xs = jnp.concatenate([x, ctx], axis=1)

    for layer in range(depth):
        need_ctx = layer < depth - 1
        mods = mods_all[layer]
        w = w_in[layer]
        wp = jnp.concatenate([w[:, :n_in], jnp.zeros((d, ga - n_in), w.dtype), w[:, n_in:]],
                             axis=1).astype(BF16)
        p = _in_proj(xs, mods, norm_mix[layer].reshape(1, d), wp, seq=seq, tm=tm_full, tn=tn)

        lb_f = _lower_bound(hg_lb_fwd, layer).reshape(1, mix)
        lb_b = _lower_bound(hg_lb_bwd, layer).reshape(1, mix)
        o_b = _hgrn(p, lb_b, seq=seq, ctx_len=ctx_len, mix=mix, rev=True, fcol=2)
        a = _hgrn(p, lb_f, seq=seq, ctx_len=ctx_len, mix=mix, rev=False, fcol=1,
                  extra=(o_b, hg_norm[layer].reshape(1, mix)))

        attn, aux = _attention(p, cos, sin, attn_sink[layer], seq=seq, ctx_len=ctx_len, mix=mix, kvw=kvw)
        if need_ctx:
            attn = _context_attention(p, attn, attn_sink[layer], aux, seq=seq, ctx_len=ctx_len, mix=mix)

        tm, nt = (tm_full, r // tm_full) if need_ctx else (tm_lat, seq // tm_lat)
        y = _merge(a, attn, p, w_branch_a[layer].astype(BF16), w_branch_b[layer].astype(BF16),
                   tm=tm, ntiles=nt, tn=tn, ga_col=ga // tn, gb_col=gb // tn)
        xs = _out_proj(y, w_out[layer].astype(BF16), xs, mods, seq=seq, tm=tm, ntiles=nt, tn=tn)

        i = layer // 2
        gain = norm_ffn[layer].reshape(1, d)
        if layer % 2 == 0:
            tf = _pick_tile(ffn_w_gate.shape[-1], 512, LANES)
            xs = _ffn(xs, mods, gain, ffn_w_gate[i].astype(BF16), ffn_w_up[i].astype(BF16),
                      ffn_w_down[i].astype(BF16), seq=seq, tm=tm_ffn, tf=tf)
        else:
            tme = _pick_tile(seq, 512)
            wr = jnp.zeros((d, LANES), F32).at[:, :n_experts].set(moe_router[i])
            h, route = _route(xs, mods, gain, wr, seq=seq, tm=tme, n_experts=n_experts)
            src, tile_expert, n_used, slot0, slot1 = _moe_plan(route, n_experts, tme)
            xg = _gather_rows(h, src, gm=_pick_tile(src.shape[0], 256))
            tf = _pick_tile(moe_w_gate.shape[-1], 512, LANES)
            yg = _moe_ffn(xg, tile_expert, n_used, moe_w_gate[i].astype(BF16),
                          moe_w_up[i].astype(BF16), moe_w_down[i].astype(BF16), tm=tme, tf=tf)
            out = _combine(xs, mods, route, final_norm.reshape(1, d), yg, slot0, slot1,
                           seq=seq, cm=_pick_tile(seq, 256))
    return out
```

```python
import functools

import jax
import jax.numpy as jnp
import numpy as np
from jax import lax
from jax.experimental import pallas as pl
from jax.experimental.pallas import tpu as pltpu

HG_HEAD_DIM = 128
ATTN_HEAD_DIM = 64
ATTN_GROUP = 8
ATTN_BLOCK = 128
GRID_W = 64
ROPE_THETA = 10000.0
EPS = 1e-6
TOP_K = 2
LANES = 128
HG_CHUNK = 128
HG_FINE = 8
VMEM_LIMIT = 56 * 1024 * 1024

F32 = jnp.float32
BF16 = jnp.bfloat16


def _pick_tile(n, cap, mult=16):
    best = None
    for t in range(mult, min(cap, n) + 1, mult):
        if n % t == 0:
            best = t
    assert best is not None, (n, cap)
    return best


def _params(sem, vmem=VMEM_LIMIT, **kw):
    return pltpu.CompilerParams(dimension_semantics=sem, vmem_limit_bytes=vmem, **kw)


def _dot(a, b):
    return jnp.dot(a, b, preferred_element_type=F32)


def _dot_nt(a, b):
    return lax.dot_general(a, b, (((1,), (1,)), ((), ())), preferred_element_type=F32)


def _dot_tn(a, b):
    return lax.dot_general(a, b, (((0,), (0,)), ((), ())), preferred_element_type=F32)


def _norm_mod(x, gain, shift, scale):
    ms = jnp.mean(x * x, axis=-1, keepdims=True)
    return x * lax.rsqrt(ms + EPS) * gain * (1.0 + scale) + shift


def _is_ctx_rows(row0, n, seq):
    rows = row0 + lax.broadcasted_iota(jnp.int32, (n, 1), 0)
    return rows >= seq


def _mod_row(modl_ref, modc_ref, k, is_ctx):
    return jnp.where(is_ctx, modc_ref[0, k:k + 1, :], modl_ref[0, k:k + 1, :])


def _for_row_chunks(tm, body):
    rc = _pick_tile(tm, 32)

    def step(k, carry):
        body(pl.multiple_of(k * rc, rc), rc)
        return carry

    lax.fori_loop(0, tm // rc, step, 0)


def _norm_mod_rows(x_ref, h_ref, g_ref, modl_ref, modc_ref, k_shift, tile_row0, tm, seq):
    def body(r0, rc):
        rows = pl.ds(r0, rc)
        is_ctx = _is_ctx_rows(tile_row0 + r0, rc, seq)
        h = _norm_mod(x_ref[0, rows, :], g_ref[...], _mod_row(modl_ref, modc_ref, k_shift, is_ctx),
                      _mod_row(modl_ref, modc_ref, k_shift + 1, is_ctx))
        h_ref[rows, :] = h.astype(h_ref.dtype)

    _for_row_chunks(tm, body)


def _mod_kernel(c_ref, w_ref, b_ref, o_ref):
    act = jax.nn.silu(c_ref[...])
    o_ref[0] = jnp.dot(act, w_ref[0], preferred_element_type=F32,
                       precision=lax.Precision.HIGHEST) + b_ref[0]


def _modulation(cvec, w_mod, b_mod):
    depth, d, n6 = w_mod.shape
    mb = cvec.shape[0]
    tn = _pick_tile(n6, 1024, LANES)
    return pl.pallas_call(
        _mod_kernel,
        grid=(depth, n6 // tn),
        in_specs=[pl.BlockSpec((mb, d), lambda l, j: (0, 0)),
                  pl.BlockSpec((1, d, tn), lambda l, j: (l, 0, j)),
                  pl.BlockSpec((1, 1, tn), lambda l, j: (l, 0, j))],
        out_specs=pl.BlockSpec((1, mb, tn), lambda l, j: (l, 0, j)),
        out_shape=jax.ShapeDtypeStruct((depth, mb, n6), F32),
        compiler_params=_params(("parallel", "arbitrary")),
        name="modulation",
    )(cvec, w_mod, b_mod.reshape(depth, 1, n6))


def _in_proj_kernel(x_ref, modl_ref, modc_ref, g_ref, w_ref, o_ref, h_ref, *, tm, seq):
    tile_row0 = pl.program_id(1) * tm

    @pl.when(pl.program_id(2) == 0)
    def _():
        _norm_mod_rows(x_ref, h_ref, g_ref, modl_ref, modc_ref, 0, tile_row0, tm, seq)

    o_ref[0] = _dot(h_ref[...], w_ref[...]).astype(o_ref.dtype)


def _in_proj(x, mods, gain, w, *, seq, tm, tn):
    b, r, d = x.shape
    npad = w.shape[1]
    nb = mods.shape[0] - 1
    return pl.pallas_call(
        functools.partial(_in_proj_kernel, tm=tm, seq=seq),
        grid=(b, r // tm, npad // tn),
        in_specs=[pl.BlockSpec((1, tm, d), lambda i, t, j: (i, t, 0)),
                  pl.BlockSpec((1, 6, d), lambda i, t, j: (i, 0, 0)),
                  pl.BlockSpec((1, 6, d), lambda i, t, j: (nb, 0, 0)),
                  pl.BlockSpec((1, d), lambda i, t, j: (0, 0)),
                  pl.BlockSpec((d, tn), lambda i, t, j: (0, j))],
        out_specs=pl.BlockSpec((1, tm, tn), lambda i, t, j: (i, t, j)),
        out_shape=jax.ShapeDtypeStruct((b, r, npad), BF16),
        scratch_shapes=[pltpu.VMEM((tm, d), BF16)],
        compiler_params=_params(("parallel", "parallel", "arbitrary")),
        name="in_proj",
    )(x, mods, mods, gain, w)


def _hgrn_tables(c, rev):
    nlev = c.bit_length() - 1
    t = np.arange(c)[:, None]
    u = np.arange(c)[None, :]
    fams = [u <= t]
    level = np.full((c, c), -1)
    for li in range(nlev):
        half = c >> (li + 1)
        blk = 2 * half
        start = (t // blk) * blk
        qside = (t % blk) >= half
        if half <= HG_FINE:
            fams.append(np.where(qside, (u >= start + half) & (u <= t), (u > t) & (u < start + half)))
        level = np.where((t // blk == u // blk) & qside & ((u % blk) < half), li, level)
    sums = np.concatenate(fams, axis=0).astype(np.float32)
    if rev:
        sums = sums.reshape(len(fams), c, c)[:, ::-1, ::-1].reshape(len(fams) * c, c)
        level = level[::-1, ::-1]
    return (jnp.asarray(np.tile(sums, (1, 2)), BF16), jnp.asarray(np.ascontiguousarray(level), jnp.int32))


def _hgrn_kernel(*refs, rev, chunk, nheads, final):
    if final:
        q_ref, f_ref, v_ref, lb_ref, sums_ref, lvl_ref, ob_ref, g_ref, gain_ref, o_ref, st_ref = refs
    else:
        q_ref, f_ref, v_ref, lb_ref, sums_ref, lvl_ref, o_ref, st_ref = refs
    c = chunk
    hd = HG_HEAD_DIM

    @pl.when(pl.program_id(1) == 0)
    def _():
        st_ref[...] = jnp.zeros_like(st_ref)

    lb = lb_ref[...]
    f = lb + (1.0 - lb) * jax.nn.sigmoid(f_ref[0].astype(F32))
    g2 = jnp.log2(f)
    g_hi = g2.astype(BF16)
    g_lo = (g2 - g_hi.astype(F32)).astype(BF16)
    expo = _dot(sums_ref[...], jnp.concatenate([g_hi, g_lo], axis=0))

    rows = lax.broadcasted_iota(jnp.int32, (c, 1), 0)
    level = lvl_ref[...]
    scale = HG_HEAD_DIM ** -0.5
    last = 0 if rev else c - 1
    nlev = c.bit_length() - 1
    in_level = [level == li for li in range(nlev)]
    outs, states = [], []

    for h in range(nheads):
        sl = slice(h * hd, (h + 1) * hd)
        qh = q_ref[0, :, sl].astype(F32) * scale
        kh = 1.0 - f[:, sl]
        vh = v_ref[0, :, sl]
        bh = expo[0:c, sl]
        btot = bh[last:last + 1, :]
        eb = jnp.exp2(bh)
        st = st_ref[h]

        scores = jnp.zeros((c, c), F32)
        fine = 0
        for li in range(nlev):
            half = c >> (li + 1)
            blk = 2 * half
            qside = ((rows & half) == 0) if rev else ((rows & half) != 0)
            if half > HG_FINE:
                ref0 = half if rev else half - 1
                bref = bh[ref0:ref0 + 1, :]
                for k in range(1, c // blk):
                    bref = jnp.where(rows >= k * blk, bh[k * blk + ref0:k * blk + ref0 + 1, :], bref)
                x = bh - bref
                x = jnp.where(qside, x, -x)
            else:
                fine += 1
                x = expo[fine * c:(fine + 1) * c, sl]
            z = (jnp.where(qside, qh, kh) * jnp.exp2(x)).astype(BF16)
            scores = jnp.where(in_level[li], _dot_nt(z, z), scores)

        lhs = jnp.concatenate([(qh * eb).astype(BF16), scores.astype(BF16)], axis=1)
        rhs = jnp.concatenate([st.T.astype(BF16), vh], axis=0)
        o = _dot(lhs, rhs) + jnp.sum(qh * kh, axis=-1, keepdims=True) * vh.astype(F32)

        khat = (kh * jnp.exp2(btot - bh)).astype(BF16)
        states.append(st * eb[last:last + 1, :] + _dot_tn(vh, khat))

        if final:
            osum = o + ob_ref[0, :, sl]
            ms = jnp.mean(osum * osum, axis=-1, keepdims=True)
            o = (osum * lax.rsqrt(ms + EPS) * gain_ref[:, sl]
                 * jax.nn.silu(g_ref[0, :, sl].astype(F32)))
        outs.append(o.astype(o_ref.dtype))

    o_ref[0] = jnp.concatenate(outs, axis=1)
    st_ref[...] = jnp.stack(states, axis=0)


def _hgrn(p, lb, *, seq, ctx_len, mix, rev, fcol, extra=None):
    b, r, _ = p.shape
    c = HG_CHUNK
    nl, nc = seq // c, ctx_len // c
    nheads = mix // HG_HEAD_DIM

    if rev:
        def order(n):
            return nl + nc - 1 - n
    else:
        def order(n):
            return jnp.where(n < nc, nl + n, n - nc)

    def pspec(col):
        return pl.BlockSpec((1, c, mix), lambda i, n: (i, order(n), col))

    sums, level = _hgrn_tables(c, rev)
    in_specs = [pspec(0), pspec(fcol), pspec(3), pl.BlockSpec((1, mix), lambda i, n: (0, 0)),
                pl.BlockSpec(sums.shape, lambda i, n: (0, 0)),
                pl.BlockSpec(level.shape, lambda i, n: (0, 0))]
    args = [p, p, p, lb, sums, level]
    final = extra is not None
    if final:
        o_other, gain = extra
        in_specs += [pl.BlockSpec((1, c, mix), lambda i, n: (i, order(n), 0)), pspec(4),
                     pl.BlockSpec((1, mix), lambda i, n: (0, 0))]
        args += [o_other, p, gain]
    return pl.pallas_call(
        functools.partial(_hgrn_kernel, rev=rev, chunk=c, nheads=nheads, final=final),
        grid=(b, nl + nc),
        in_specs=in_specs,
        out_specs=pl.BlockSpec((1, c, mix), lambda i, n: (i, order(n), 0)),
        out_shape=jax.ShapeDtypeStruct((b, r, mix), BF16 if final else F32),
        scratch_shapes=[pltpu.VMEM((nheads, HG_HEAD_DIM, HG_HEAD_DIM), F32)],
        compiler_params=_params(("parallel", "arbitrary")),
        name="hgrn_bwd" if rev else "hgrn_fwd",
    )(*args)


def _rope(x, cos, sin):
    lane = lax.broadcasted_iota(jnp.int32, x.shape, 1)
    first = (lane % 32) < 16
    swapped = jnp.where(first, pltpu.roll(x, LANES - 16, 1), pltpu.roll(x, 16, 1))
    return x * cos + swapped * sin


def _attn_kernel(*refs, windowed, nblk, kvh, ctx_len):
    if windowed:
        (q_ref, kp_ref, kc_ref, kn_ref, vp_ref, vc_ref, vn_ref, kx_ref, vx_ref,
         cq_ref, sq_ref, cp_ref, sp_ref, cn_ref, sn_ref, sink_ref, o_ref) = refs
    else:
        q_ref, kx_ref, vx_ref, sink_ref, o_ref = refs
    blk = ATTN_BLOCK
    hd = ATTN_HEAD_DIM
    gw = ATTN_GROUP * hd
    n = pl.program_id(1)
    log2e = 1.4426950408889634
    scale = ATTN_HEAD_DIM ** -0.5 * log2e

    if windowed:
        kwin = [_rope(kp_ref[0].astype(F32), cp_ref[...], sp_ref[...]),
                _rope(kc_ref[0].astype(F32), cq_ref[...], sq_ref[...]),
                _rope(kn_ref[0].astype(F32), cn_ref[...], sn_ref[...])]
        vwin = [vp_ref[0], vc_ref[0], vn_ref[0]]
        nkeys = ctx_len + 3 * blk
        qi = lax.broadcasted_iota(jnp.int32, (blk, nkeys), 0)
        kj = lax.broadcasted_iota(jnp.int32, (blk, nkeys), 1) - ctx_len
        valid = (kj < 0) | ((kj >= qi) & (kj <= qi + 2 * blk)
                            & ((kj >= blk) | (n > 0)) & ((kj < 2 * blk) | (n < nblk - 1)))

    for g in range(kvh):
        ks = slice(g * hd, (g + 1) * hd)
        k_parts = [kx_ref[0, :, ks]]
        v_parts = [vx_ref[0, :, ks]]
        if windowed:
            k_parts += [kw[:, ks] for kw in kwin]
            v_parts += [vw[:, ks] for vw in vwin]
        k_all = jnp.concatenate(k_parts, axis=0).astype(BF16)
        v_all = jnp.concatenate(v_parts, axis=0).astype(BF16)

        q_heads = []
        for c4 in range(gw // LANES):
            col = g * gw + c4 * LANES
            qc = q_ref[0, :, col:col + LANES].astype(F32)
            if windowed:
                qc = _rope(qc, cq_ref[...], sq_ref[...])
            qc = qc * scale
            q_heads += [qc[:, :hd], qc[:, hd:]]
        q_stack = jnp.concatenate(q_heads, axis=0).astype(BF16)
        s_all = _dot_nt(q_stack, k_all)

        p_list, inv_list = [], []
        for r in range(ATTN_GROUP):
            s = s_all[r * blk:(r + 1) * blk, :]
            if windowed:
                s = jnp.where(valid, s, -jnp.inf)
            sink = sink_ref[g * ATTN_GROUP + r] * log2e
            m = jnp.maximum(jnp.max(s, axis=-1, keepdims=True), sink)
            p = jnp.exp2(s - m)
            denom = jnp.sum(p, axis=-1, keepdims=True) + jnp.exp2(sink - m)
            p_list.append(p.astype(BF16))
            inv_list.append(1.0 / denom)
        o_all = _dot(jnp.concatenate(p_list, axis=0), v_all)
        outs = [o_all[r * blk:(r + 1) * blk, :] * inv_list[r] for r in range(ATTN_GROUP)]
        o_ref[0, :, g * gw:(g + 1) * gw] = jnp.concatenate(outs, axis=1).astype(o_ref.dtype)


def _attention(p, cos, sin, sink, *, seq, ctx_len, mix, kvw):
    b, r, _ = p.shape
    blk = ATTN_BLOCK
    nblk = seq // blk
    kvh = kvw // ATTN_HEAD_DIM
    kcol = 6 * mix // kvw
    ctx_blk = seq // ctx_len

    def kv(col, off):
        return pl.BlockSpec((1, blk, kvw),
                            lambda i, n: (i, jnp.clip(n + off, 0, nblk - 1), col))

    def tab(off):
        return pl.BlockSpec((blk, LANES), lambda i, n: (jnp.clip(n + off, 0, nblk - 1), 0))

    smem = pl.BlockSpec(memory_space=pltpu.SMEM)
    ctx_k = pl.BlockSpec((1, ctx_len, kvw), lambda i, n: (i, ctx_blk, kcol))
    ctx_v = pl.BlockSpec((1, ctx_len, kvw), lambda i, n: (i, ctx_blk, kcol + 1))
    out_shape = jax.ShapeDtypeStruct((b, r, mix), BF16)

    lat = pl.pallas_call(
        functools.partial(_attn_window_kernel, windowed=True, nblk=nblk, kvh=kvh, ctx_len=ctx_len),
        grid=(b, r // blk),
        in_specs=[pl.BlockSpec((1, blk, mix), lambda i, n: (i, jnp.minimum(n, nblk - 1), 5)),
                  kv(kcol, -1), kv(kcol, 0), kv(kcol, 1),
                  kv(kcol + 1, -1), kv(kcol + 1, 0), kv(kcol + 1, 1),
                  ctx_k, ctx_v,
                  tab(0), tab(0), tab(-1), tab(-1), tab(1), tab(1), smem],
        out_specs=pl.BlockSpec((1, blk, mix), lambda i, n: (i, n, 0)),
        out_shape=out_shape,
        compiler_params=_params(("parallel", "arbitrary")),
        name="attn_window",
    )(p, p, p, p, p, p, p, p, p, cos, sin, cos, sin, cos, sin, sink)
    return lat, (ctx_k, ctx_v, smem, out_shape, kvh)


def _context_attention(p, attn_lat, sink, aux, *, seq, ctx_len, mix):
    ctx_k, ctx_v, smem, out_shape, kvh = aux
    b = p.shape[0]
    blk = ATTN_BLOCK
    base = seq // blk
    return pl.pallas_call(
        functools.partial(_attn_ctx_kernel, windowed=False, nblk=0, kvh=kvh, ctx_len=ctx_len),
        grid=(b, ctx_len // blk),
        in_specs=[pl.BlockSpec((1, blk, mix), lambda i, n: (i, base + n, 5)),
                  ctx_k, ctx_v, smem,
                  pl.BlockSpec(memory_space=pl.ANY)],
        out_specs=pl.BlockSpec((1, blk, mix), lambda i, n: (i, base + n, 0)),
        out_shape=out_shape,
        input_output_aliases={4: 0},
        compiler_params=_params(("parallel", "arbitrary")),
        name="attn_context",
    )(p, p, p, sink, attn_lat)


def _attn_ctx_kernel(q_ref, kx_ref, vx_ref, sink_ref, prev_ref, o_ref, **kw):
    del prev_ref
    _attn_kernel(q_ref, kx_ref, vx_ref, sink_ref, o_ref, **kw)


def _attn_window_kernel(*refs, nblk, **kw):
    n = pl.program_id(1)
    o_ref = refs[-1]

    @pl.when(n < nblk)
    def _():
        _attn_kernel(*refs, nblk=nblk, **kw)

    @pl.when(n >= nblk)
    def _():
        o_ref[...] = jnp.zeros_like(o_ref)


def _merge_kernel(a_ref, t_ref, ga_ref, gb_ref, wa_ref, wb_ref, o_ref):
    ya = _dot(a_ref[0], wa_ref[...])
    yb = _dot(t_ref[0], wb_ref[...])
    y = (jax.nn.sigmoid(ga_ref[0].astype(F32)) * ya
         + jax.nn.sigmoid(gb_ref[0].astype(F32)) * yb)
    o_ref[0] = y.astype(o_ref.dtype)


def _merge(a, attn, p, wa, wb, *, tm, ntiles, tn, ga_col, gb_col):
    b, r, mix = a.shape
    d = wa.shape[1]
    return pl.pallas_call(
        _merge_kernel,
        grid=(b, ntiles, d // tn),
        in_specs=[pl.BlockSpec((1, tm, mix), lambda i, t, j: (i, t, 0)),
                  pl.BlockSpec((1, tm, mix), lambda i, t, j: (i, t, 0)),
                  pl.BlockSpec((1, tm, tn), lambda i, t, j: (i, t, ga_col + j)),
                  pl.BlockSpec((1, tm, tn), lambda i, t, j: (i, t, gb_col + j)),
                  pl.BlockSpec((mix, tn), lambda i, t, j: (0, j)),
                  pl.BlockSpec((mix, tn), lambda i, t, j: (0, j))],
        out_specs=pl.BlockSpec((1, tm, tn), lambda i, t, j: (i, t, j)),
        out_shape=jax.ShapeDtypeStruct((b, ntiles * tm, d), BF16),
        compiler_params=_params(("parallel", "parallel", "arbitrary")),
        name="branch_merge",
    )(a, attn, p, p, wa, wb)


def _out_proj_kernel(y_ref, w_ref, x_ref, modl_ref, modc_ref, o_ref, *, tm, seq):
    is_ctx = _is_ctx_rows(pl.program_id(1) * tm, tm, seq)
    gate = jnp.where(is_ctx, modc_ref[0, 2:3, :], modl_ref[0, 2:3, :])
    o_ref[0] = x_ref[0] + gate * _dot(y_ref[0], w_ref[...])


def _out_proj(y, w, x, mods, *, seq, tm, ntiles, tn):
    b, r, d = x.shape
    nb = mods.shape[0] - 1
    return pl.pallas_call(
        functools.partial(_out_proj_kernel, tm=tm, seq=seq),
        grid=(b, ntiles, d // tn),
        in_specs=[pl.BlockSpec((1, tm, d), lambda i, t, j: (i, t, 0)),
                  pl.BlockSpec((d, tn), lambda i, t, j: (0, j)),
                  pl.BlockSpec((1, tm, tn), lambda i, t, j: (i, t, j)),
                  pl.BlockSpec((1, 6, tn), lambda i, t, j: (i, 0, j)),
                  pl.BlockSpec((1, 6, tn), lambda i, t, j: (nb, 0, j))],
        out_specs=pl.BlockSpec((1, tm, tn), lambda i, t, j: (i, t, j)),
        out_shape=jax.ShapeDtypeStruct((b, ntiles * tm, d), F32),
        compiler_params=_params(("parallel", "parallel", "arbitrary")),
        name="out_proj",
    )(y, w, x, mods, mods)


def _ffn_kernel(x_ref, modl_ref, modc_ref, g_ref, wg_ref, wu_ref, wd_ref, o_ref, h_ref, acc_ref,
                *, tm, seq):
    j = pl.program_id(2)
    last = pl.num_programs(2) - 1
    tile_row0 = pl.program_id(1) * tm

    @pl.when(j == 0)
    def _():
        _norm_mod_rows(x_ref, h_ref, g_ref, modl_ref, modc_ref, 3, tile_row0, tm, seq)
        acc_ref[...] = jnp.zeros_like(acc_ref)

    h = h_ref[...]
    act = jax.nn.silu(_dot(h, wg_ref[...])) * _dot(h, wu_ref[...])
    acc_ref[...] += _dot(act.astype(BF16), wd_ref[...])

    @pl.when(j == last)
    def _():
        def body(r0, rc):
            rows = pl.ds(r0, rc)
            is_ctx = _is_ctx_rows(tile_row0 + r0, rc, seq)
            o_ref[0, rows, :] = (x_ref[0, rows, :]
                                 + _mod_row(modl_ref, modc_ref, 5, is_ctx) * acc_ref[rows, :])

        _for_row_chunks(tm, body)


def _ffn(x, mods, gain, wg, wu, wd, *, seq, tm, tf):
    b, r, d = x.shape
    f = wg.shape[1]
    nb = mods.shape[0] - 1
    return pl.pallas_call(
        functools.partial(_ffn_kernel, tm=tm, seq=seq),
        grid=(b, r // tm, f // tf),
        in_specs=[pl.BlockSpec((1, tm, d), lambda i, t, j: (i, t, 0)),
                  pl.BlockSpec((1, 6, d), lambda i, t, j: (i, 0, 0)),
                  pl.BlockSpec((1, 6, d), lambda i, t, j: (nb, 0, 0)),
                  pl.BlockSpec((1, d), lambda i, t, j: (0, 0)),
                  pl.BlockSpec((d, tf), lambda i, t, j: (0, j)),
                  pl.BlockSpec((d, tf), lambda i, t, j: (0, j)),
                  pl.BlockSpec((tf, d), lambda i, t, j: (j, 0))],
        out_specs=pl.BlockSpec((1, tm, d), lambda i, t, j: (i, t, 0)),
        out_shape=jax.ShapeDtypeStruct((b, r, d), F32),
        scratch_shapes=[pltpu.VMEM((tm, d), BF16), pltpu.VMEM((tm, d), F32)],
        compiler_params=_params(("parallel", "parallel", "arbitrary")),
        name="ffn_dense",
    )(x, mods, mods, gain, wg, wu, wd)


def _route_kernel(x_ref, modl_ref, g_ref, wr_ref, h_ref, r_ref, *, n_experts, tm):
    def body(r0, rc):
        rows = pl.ds(r0, rc)
        h = _norm_mod(x_ref[0, rows, :], g_ref[...], modl_ref[0, 3:4, :], modl_ref[0, 4:5, :])
        h_ref[rows, :] = h
        logits = jnp.dot(h, wr_ref[...], preferred_element_type=F32, precision=lax.Precision.HIGHEST)
        lane = lax.broadcasted_iota(jnp.int32, logits.shape, 1)
        lg = jnp.where(lane < n_experts, logits, -jnp.inf)
        m1 = jnp.max(lg, axis=-1, keepdims=True)
        i1 = jnp.min(jnp.where(lg == m1, lane, LANES), axis=-1, keepdims=True)
        lg2 = jnp.where(lane == i1, -jnp.inf, lg)
        m2 = jnp.max(lg2, axis=-1, keepdims=True)
        i2 = jnp.min(jnp.where(lg2 == m2, lane, LANES), axis=-1, keepdims=True)
        e2 = jnp.exp(m2 - m1)
        w1 = 1.0 / (1.0 + e2)
        w2 = e2 / (1.0 + e2)
        r_ref[rows, :] = jnp.where(lane == 0, i1.astype(F32),
                                   jnp.where(lane == 1, i2.astype(F32),
                                             jnp.where(lane == 2, w1, jnp.where(lane == 3, w2, 0.0))))

    _for_row_chunks(tm, body)


def _route(x, mods, gain, wr, *, seq, tm, n_experts):
    b, r, d = x.shape
    nt = seq // tm
    return pl.pallas_call(
        functools.partial(_route_kernel, n_experts=n_experts, tm=tm),
        grid=(b, nt),
        in_specs=[pl.BlockSpec((1, tm, d), lambda i, t: (i, t, 0)),
                  pl.BlockSpec((1, 6, d), lambda i, t: (i, 0, 0)),
                  pl.BlockSpec((1, d), lambda i, t: (0, 0)),
                  pl.BlockSpec((d, LANES), lambda i, t: (0, 0))],
        out_specs=[pl.BlockSpec((tm, d), lambda i, t: (i * nt + t, 0)),
                   pl.BlockSpec((tm, LANES), lambda i, t: (i * nt + t, 0))],
        out_shape=[jax.ShapeDtypeStruct((b * seq, d), F32),
                   jax.ShapeDtypeStruct((b * seq, LANES), F32)],
        compiler_params=_params(("parallel", "parallel")),
        name="moe_route",
    )(x, mods, gain, wr)


def _gather_kernel(idx_ref, src_ref, o_ref, sem, *, gm):
    def issue(r, carry):
        pltpu.make_async_copy(src_ref.at[pl.ds(idx_ref[0, 0, r], 1), :],
                              o_ref.at[pl.ds(r, 1), :], sem).start()
        return carry

    lax.fori_loop(0, gm, issue, 0)

    def drain(r, carry):
        pltpu.make_async_copy(src_ref.at[pl.ds(0, 1), :], o_ref.at[pl.ds(r, 1), :], sem).wait()
        return carry

    lax.fori_loop(0, gm, drain, 0)


def _gather_rows(src, idx, *, gm):
    n = idx.shape[0]
    d = src.shape[1]
    return pl.pallas_call(
        functools.partial(_gather_kernel, gm=gm),
        grid=(n // gm,),
        in_specs=[pl.BlockSpec((1, 1, gm), lambda i: (i, 0, 0), memory_space=pltpu.SMEM),
                  pl.BlockSpec(memory_space=pl.ANY)],
        out_specs=pl.BlockSpec((gm, d), lambda i: (i, 0)),
        out_shape=jax.ShapeDtypeStruct((n, d), src.dtype),
        scratch_shapes=[pltpu.SemaphoreType.DMA(())],
        compiler_params=_params(("arbitrary",)),
        name="moe_gather",
    )(idx.reshape(n // gm, 1, gm), src)


def _moe_ffn_kernel(te_ref, nu_ref, x_ref, wg_ref, wu_ref, wd_ref, o_ref, h_ref, acc_ref):
    i, j = pl.program_id(0), pl.program_id(1)
    last = pl.num_programs(1) - 1
    used = i < nu_ref[0]

    @pl.when(j == 0)
    def _():
        h_ref[...] = x_ref[...].astype(BF16)
        acc_ref[...] = jnp.zeros_like(acc_ref)

    @pl.when(used)
    def _():
        h = h_ref[...]
        act = jax.nn.silu(_dot(h, wg_ref[0])) * _dot(h, wu_ref[0])
        acc_ref[...] += _dot(act.astype(BF16), wd_ref[0])

    @pl.when(j == last)
    def _():
        o_ref[...] = acc_ref[...]


def _moe_ffn(xs, tile_expert, n_used, wg, wu, wd, *, tm, tf):
    ns, d = xs.shape
    f = wg.shape[2]

    def row(i, j, te, nu):
        return (jnp.minimum(i, nu[0] - 1), 0)

    def w_up(i, j, te, nu):
        return (te[i], 0, jnp.where(i < nu[0], j, 0))

    def w_down(i, j, te, nu):
        return (te[i], jnp.where(i < nu[0], j, 0), 0)

    grid_spec = pltpu.PrefetchScalarGridSpec(
        num_scalar_prefetch=2,
        grid=(ns // tm, f // tf),
        in_specs=[pl.BlockSpec((tm, d), row),
                  pl.BlockSpec((1, d, tf), w_up),
                  pl.BlockSpec((1, d, tf), w_up),
                  pl.BlockSpec((1, tf, d), w_down)],
        out_specs=pl.BlockSpec((tm, d), lambda i, j, te, nu: (i, 0)),
        scratch_shapes=[pltpu.VMEM((tm, d), BF16), pltpu.VMEM((tm, d), F32)])
    return pl.pallas_call(
        _moe_ffn_kernel,
        grid_spec=grid_spec,
        out_shape=jax.ShapeDtypeStruct((ns, d), F32),
        compiler_params=_params(("arbitrary", "arbitrary")),
        name="moe_ffn",
    )(tile_expert, n_used, xs, wg, wu, wd)


def _combine_kernel(s0_ref, s1_ref, x_ref, modl_ref, r_ref, fn_ref, y_ref, o_ref, ya_ref, yb_ref, sem,
                    *, cm):
    def issue(r, carry):
        pltpu.make_async_copy(y_ref.at[pl.ds(s0_ref[0, 0, r], 1), :],
                              ya_ref.at[pl.ds(r, 1), :], sem.at[0]).start()
        pltpu.make_async_copy(y_ref.at[pl.ds(s1_ref[0, 0, r], 1), :],
                              yb_ref.at[pl.ds(r, 1), :], sem.at[1]).start()
        return carry

    lax.fori_loop(0, cm, issue, 0)

    def drain(r, carry):
        pltpu.make_async_copy(y_ref.at[pl.ds(0, 1), :], ya_ref.at[pl.ds(r, 1), :], sem.at[0]).wait()
        pltpu.make_async_copy(y_ref.at[pl.ds(0, 1), :], yb_ref.at[pl.ds(r, 1), :], sem.at[1]).wait()
        return carry

    lax.fori_loop(0, cm, drain, 0)

    route = r_ref[...]
    moe = route[:, 2:3] * ya_ref[...] + route[:, 3:4] * yb_ref[...]
    xn = x_ref[0] + modl_ref[0, 5:6, :] * moe
    ms = jnp.mean(xn * xn, axis=-1, keepdims=True)
    o_ref[0] = xn * lax.rsqrt(ms + EPS) * fn_ref[...]


def _combine(x, mods, route, final_gain, y, slot0, slot1, *, seq, cm):
    b, r, d = x.shape
    nt = seq // cm

    def sidx():
        return pl.BlockSpec((1, 1, cm), lambda i, t: (i * nt + t, 0, 0), memory_space=pltpu.SMEM)

    return pl.pallas_call(
        functools.partial(_combine_kernel, cm=cm),
        grid=(b, nt),
        in_specs=[sidx(), sidx(),
                  pl.BlockSpec((1, cm, d), lambda i, t: (i, t, 0)),
                  pl.BlockSpec((1, 6, d), lambda i, t: (i, 0, 0)),
                  pl.BlockSpec((cm, LANES), lambda i, t: (i * nt + t, 0)),
                  pl.BlockSpec((1, d), lambda i, t: (0, 0)),
                  pl.BlockSpec(memory_space=pl.ANY)],
        out_specs=pl.BlockSpec((1, cm, d), lambda i, t: (i, t, 0)),
        out_shape=jax.ShapeDtypeStruct((b, seq, d), F32),
        scratch_shapes=[pltpu.VMEM((cm, d), F32), pltpu.VMEM((cm, d), F32),
                        pltpu.SemaphoreType.DMA((2,))],
        compiler_params=_params(("arbitrary", "arbitrary")),
        name="moe_combine",
    )(slot0.reshape(b * nt, 1, cm), slot1.reshape(b * nt, 1, cm), x, mods, route, final_gain, y)


def _moe_plan(route, n_experts, tm):
    t = route.shape[0]
    eid = route[:, :TOP_K].astype(jnp.int32).reshape(-1)
    onehot = (eid[:, None] == jnp.arange(n_experts, dtype=jnp.int32)[None, :]).astype(jnp.int32)
    csum = jnp.cumsum(onehot, axis=0)
    rank = jnp.sum((csum - onehot) * onehot, axis=1)
    counts = csum[-1]
    padded = ((counts + tm - 1) // tm) * tm
    ends = jnp.cumsum(padded)
    starts = ends - padded
    slot = starts[eid] + rank
    ns = TOP_K * t + n_experts * tm
    token = jnp.arange(TOP_K * t, dtype=jnp.int32) // TOP_K
    src = jnp.zeros((ns,), jnp.int32).at[slot].set(token)
    tile_start = jnp.arange(ns // tm, dtype=jnp.int32) * tm
    tile_expert = jnp.minimum(jnp.sum((tile_start[:, None] >= ends[None, :]).astype(jnp.int32), axis=1),
                              n_experts - 1)
    n_used = (ends[-1] // tm).astype(jnp.int32).reshape(1)
    slots = slot.reshape(t, TOP_K)
    return src, tile_expert, n_used, slots[:, 0], slots[:, 1]


def _rope_tables(length):
    rows = length // GRID_W
    row = jnp.repeat(jnp.arange(rows, dtype=F32), GRID_W)
    col = jnp.tile(jnp.arange(GRID_W, dtype=F32), rows)
    n_freq = ATTN_HEAD_DIM // 4
    inv = ROPE_THETA ** (-jnp.arange(n_freq, dtype=F32) / n_freq)
    ar, ac = row[:, None] * inv, col[:, None] * inv
    cos = jnp.concatenate([jnp.cos(ar), jnp.cos(ar), jnp.cos(ac), jnp.cos(ac)], axis=1)
    sin = jnp.concatenate([-jnp.sin(ar), jnp.sin(ar), -jnp.sin(ac), jnp.sin(ac)], axis=1)
    reps = LANES // ATTN_HEAD_DIM
    return jnp.tile(cos, (1, reps)), jnp.tile(sin, (1, reps))


def _lower_bound(raw, layer):
    p = jax.nn.softmax(raw.astype(F32), axis=0)
    return (jnp.cumsum(p, axis=0) - p[0])[layer]


def kernel(x, c, ctx, c_ctx, w_mod, b_mod, norm_mix, norm_ffn, w_in, hg_lb_fwd, hg_lb_bwd, hg_norm,
           attn_sink, w_branch_a, w_branch_b, w_out, ffn_w_gate, ffn_w_up, ffn_w_down,
           moe_router, moe_w_gate, moe_w_up, moe_w_down, final_norm):
    b, seq, d = x.shape
    ctx_len = ctx.shape[1]
    depth = w_mod.shape[0]
    r = seq + ctx_len
    mix = d // 2
    kvw = (mix // ATTN_HEAD_DIM // ATTN_GROUP) * ATTN_HEAD_DIM
    n_experts = moe_router.shape[-1]
    assert depth == 2 and kvw % LANES == 0 and seq % ctx_len == 0
    assert seq % HG_CHUNK == 0 and ctx_len % HG_CHUNK == 0 and seq % GRID_W == 0

    tn = 512
    ga = -(-(6 * mix + 2 * kvw) // tn) * tn
    gb = ga + d
    n_in = 6 * mix + 2 * kvw
    tm_full = _pick_tile(r, 1152)
    tm_lat = _pick_tile(seq, 1024)
    tm_ffn = _pick_tile(r, 576)

    mb = -(-(b + 1) // 8) * 8
    cvec = jnp.zeros((mb, d), F32).at[:b].set(c).at[b].set(c_ctx)
    mods_all = _modulation(cvec, w_mod, b_mod)[:, :b + 1].reshape(depth, b + 1, 6, d)

    cos, sin = _rope_tables(seq)
    xs = jnp.concatenate([x, ctx], axis=1)

    for layer in range(depth):
        need_ctx = layer < depth - 1
        mods = mods_all[layer]
        w = w_in[layer]
        wp = jnp.concatenate([w[:, :n_in], jnp.zeros((d, ga - n_in), w.dtype), w[:, n_in:]],
                             axis=1).astype(BF16)
        p = _in_proj(xs, mods, norm_mix[layer].reshape(1, d), wp, seq=seq, tm=tm_full, tn=tn)

        lb_f = _lower_bound(hg_lb_fwd, layer).reshape(1, mix)
        lb_b = _lower_bound(hg_lb_bwd, layer).reshape(1, mix)
        o_b = _hgrn(p, lb_b, seq=seq, ctx_len=ctx_len, mix=mix, rev=True, fcol=2)
        a = _hgrn(p, lb_f, seq=seq, ctx_len=ctx_len, mix=mix, rev=False, fcol=1,
                  extra=(o_b, hg_norm[layer].reshape(1, mix)))

        attn, aux = _attention(p, cos, sin, attn_sink[layer], seq=seq, ctx_len=ctx_len, mix=mix, kvw=kvw)
        if need_ctx:
            attn = _context_attention(p, attn, attn_sink[layer], aux, seq=seq, ctx_len=ctx_len, mix=mix)

        tm, nt = (tm_full, r // tm_full) if need_ctx else (tm_lat, seq // tm_lat)
        y = _merge(a, attn, p, w_branch_a[layer].astype(BF16), w_branch_b[layer].astype(BF16),
                   tm=tm, ntiles=nt, tn=tn, ga_col=ga // tn, gb_col=gb // tn)
        xs = _out_proj(y, w_out[layer].astype(BF16), xs, mods, seq=seq, tm=tm, ntiles=nt, tn=tn)

        i = layer // 2
        gain = norm_ffn[layer].reshape(1, d)
        if layer % 2 == 0:
            tf = _pick_tile(ffn_w_gate.shape[-1], 512, LANES)
            xs = _ffn(xs, mods, gain, ffn_w_gate[i].astype(BF16), ffn_w_up[i].astype(BF16),
                      ffn_w_down[i].astype(BF16), seq=seq, tm=tm_ffn, tf=tf)
        else:
            tme = _pick_tile(seq, 512)
            wr = jnp.zeros((d, LANES), F32).at[:, :n_experts].set(moe_router[i])
            h, route = _route(xs, mods, gain, wr, seq=seq, tm=tme, n_experts=n_experts)
            src, tile_expert, n_used, slot0, slot1 = _moe_plan(route, n_experts, tme)
            xg = _gather_rows(h, src, gm=_pick_tile(src.shape[0], 256))
            tf = _pick_tile(moe_w_gate.shape[-1], 1024, LANES)
            yg = _moe_ffn(xg, tile_expert, n_used, moe_w_gate[i].astype(BF16),
                          moe_w_up[i].astype(BF16), moe_w_down[i].astype(BF16), tm=tme, tf=tf)
            out = _combine(xs, mods, route, final_norm.reshape(1, d), yg, slot0, slot1,
                           seq=seq, cm=_pick_tile(seq, 256))
    return out
```

```python
import functools

import jax
import jax.numpy as jnp
import numpy as np
from jax import lax
from jax.experimental import pallas as pl
from jax.experimental.pallas import tpu as pltpu

HG_HEAD_DIM = 128
ATTN_HEAD_DIM = 64
ATTN_GROUP = 8
ATTN_BLOCK = 128
GRID_W = 64
ROPE_THETA = 10000.0
EPS = 1e-6
TOP_K = 2
LANES = 128
HG_CHUNK = 128
HG_FINE = 8
VMEM_LIMIT = 56 * 1024 * 1024

F32 = jnp.float32
BF16 = jnp.bfloat16


def _pick_tile(n, cap, mult=16):
    best = None
    for t in range(mult, min(cap, n) + 1, mult):
        if n % t == 0:
            best = t
    assert best is not None, (n, cap)
    return best


def _params(sem, vmem=VMEM_LIMIT, **kw):
    return pltpu.CompilerParams(dimension_semantics=sem, vmem_limit_bytes=vmem, **kw)


def _dot(a, b):
    return jnp.dot(a, b, preferred_element_type=F32)


def _dot_nt(a, b):
    return lax.dot_general(a, b, (((1,), (1,)), ((), ())), preferred_element_type=F32)


def _dot_tn(a, b):
    return lax.dot_general(a, b, (((0,), (0,)), ((), ())), preferred_element_type=F32)


def _norm_mod(x, gain, shift, scale):
    ms = jnp.mean(x * x, axis=-1, keepdims=True)
    return x * lax.rsqrt(ms + EPS) * gain * (1.0 + scale) + shift


def _is_ctx_rows(row0, n, seq):
    rows = row0 + lax.broadcasted_iota(jnp.int32, (n, 1), 0)
    return rows >= seq


def _mod_row(modl_ref, modc_ref, k, is_ctx):
    return jnp.where(is_ctx, modc_ref[0, k:k + 1, :], modl_ref[0, k:k + 1, :])


def _for_row_chunks(tm, body):
    rc = _pick_tile(tm, 32)

    def step(k, carry):
        body(pl.multiple_of(k * rc, rc), rc)
        return carry

    lax.fori_loop(0, tm // rc, step, 0)


def _norm_mod_rows(x_ref, h_ref, g_ref, modl_ref, modc_ref, k_shift, tile_row0, tm, seq):
    def body(r0, rc):
        rows = pl.ds(r0, rc)
        is_ctx = _is_ctx_rows(tile_row0 + r0, rc, seq)
        h = _norm_mod(x_ref[0, rows, :], g_ref[...], _mod_row(modl_ref, modc_ref, k_shift, is_ctx),
                      _mod_row(modl_ref, modc_ref, k_shift + 1, is_ctx))
        h_ref[rows, :] = h.astype(h_ref.dtype)

    _for_row_chunks(tm, body)


def _mod_kernel(c_ref, w_ref, b_ref, o_ref):
    act = jax.nn.silu(c_ref[...])
    o_ref[0] = jnp.dot(act, w_ref[0], preferred_element_type=F32,
                       precision=lax.Precision.HIGHEST) + b_ref[0]


def _modulation(cvec, w_mod, b_mod):
    depth, d, n6 = w_mod.shape
    mb = cvec.shape[0]
    tn = _pick_tile(n6, 1024, LANES)
    return pl.pallas_call(
        _mod_kernel,
        grid=(depth, n6 // tn),
        in_specs=[pl.BlockSpec((mb, d), lambda l, j: (0, 0)),
                  pl.BlockSpec((1, d, tn), lambda l, j: (l, 0, j)),
                  pl.BlockSpec((1, 1, tn), lambda l, j: (l, 0, j))],
        out_specs=pl.BlockSpec((1, mb, tn), lambda l, j: (l, 0, j)),
        out_shape=jax.ShapeDtypeStruct((depth, mb, n6), F32),
        compiler_params=_params(("parallel", "arbitrary")),
        name="modulation",
    )(cvec, w_mod, b_mod.reshape(depth, 1, n6))


def _in_proj_kernel(x_ref, modl_ref, modc_ref, g_ref, w_ref, o_ref, h_ref, *, tm, seq):
    tile_row0 = pl.program_id(1) * tm

    @pl.when(pl.program_id(2) == 0)
    def _():
        _norm_mod_rows(x_ref, h_ref, g_ref, modl_ref, modc_ref, 0, tile_row0, tm, seq)

    o_ref[0] = _dot(h_ref[...], w_ref[...]).astype(o_ref.dtype)


def _in_proj(x, mods, gain, w, *, seq, tm, tn):
    b, r, d = x.shape
    npad = w.shape[1]
    nb = mods.shape[0] - 1
    return pl.pallas_call(
        functools.partial(_in_proj_kernel, tm=tm, seq=seq),
        grid=(b, r // tm, npad // tn),
        in_specs=[pl.BlockSpec((1, tm, d), lambda i, t, j: (i, t, 0)),
                  pl.BlockSpec((1, 6, d), lambda i, t, j: (i, 0, 0)),
                  pl.BlockSpec((1, 6, d), lambda i, t, j: (nb, 0, 0)),
                  pl.BlockSpec((1, d), lambda i, t, j: (0, 0)),
                  pl.BlockSpec((d, tn), lambda i, t, j: (0, j))],
        out_specs=pl.BlockSpec((1, tm, tn), lambda i, t, j: (i, t, j)),
        out_shape=jax.ShapeDtypeStruct((b, r, npad), BF16),
        scratch_shapes=[pltpu.VMEM((tm, d), BF16)],
        compiler_params=_params(("parallel", "parallel", "arbitrary")),
        name="in_proj",
    )(x, mods, mods, gain, w)


def _hgrn_tables(c, rev):
    nlev = c.bit_length() - 1
    t = np.arange(c)[:, None]
    u = np.arange(c)[None, :]
    fams = [u <= t]
    level = np.full((c, c), -1)
    for li in range(nlev):
        half = c >> (li + 1)
        blk = 2 * half
        start = (t // blk) * blk
        qside = (t % blk) >= half
        if half <= HG_FINE:
            fams.append(np.where(qside, (u >= start + half) & (u <= t), (u > t) & (u < start + half)))
        level = np.where((t // blk == u // blk) & qside & ((u % blk) < half), li, level)
    sums = np.concatenate(fams, axis=0).astype(np.float32)
    if rev:
        sums = sums.reshape(len(fams), c, c)[:, ::-1, ::-1].reshape(len(fams) * c, c)
        level = level[::-1, ::-1]
    return (jnp.asarray(np.tile(sums, (1, 2)), BF16), jnp.asarray(np.ascontiguousarray(level), jnp.int32))


def _hgrn_kernel(*refs, rev, chunk, nheads, final):
    if final:
        q_ref, f_ref, v_ref, lb_ref, sums_ref, lvl_ref, ob_ref, g_ref, gain_ref, o_ref, st_ref = refs
    else:
        q_ref, f_ref, v_ref, lb_ref, sums_ref, lvl_ref, o_ref, st_ref = refs
    c = chunk
    hd = HG_HEAD_DIM

    @pl.when(pl.program_id(1) == 0)
    def _():
        st_ref[...] = jnp.zeros_like(st_ref)

    lb = lb_ref[...]
    f = lb + (1.0 - lb) * jax.nn.sigmoid(f_ref[0].astype(F32))
    g2 = jnp.log2(f)
    g_hi = g2.astype(BF16)
    g_lo = (g2 - g_hi.astype(F32)).astype(BF16)
    expo = _dot(sums_ref[...], jnp.concatenate([g_hi, g_lo], axis=0))

    rows = lax.broadcasted_iota(jnp.int32, (c, 1), 0)
    level = lvl_ref[...]
    scale = HG_HEAD_DIM ** -0.5
    last = 0 if rev else c - 1
    nlev = c.bit_length() - 1
    in_level = [level == li for li in range(nlev)]
    outs, states = [], []

    for h in range(nheads):
        sl = slice(h * hd, (h + 1) * hd)
        qh = q_ref[0, :, sl].astype(F32) * scale
        kh = 1.0 - f[:, sl]
        vh = v_ref[0, :, sl]
        bh = expo[0:c, sl]
        btot = bh[last:last + 1, :]
        eb = jnp.exp2(bh)
        st = st_ref[h]

        scores = jnp.zeros((c, c), F32)
        fine = 0
        for li in range(nlev):
            half = c >> (li + 1)
            blk = 2 * half
            qside = ((rows & half) == 0) if rev else ((rows & half) != 0)
            if half > HG_FINE:
                ref0 = half if rev else half - 1
                bref = bh[ref0:ref0 + 1, :]
                for k in range(1, c // blk):
                    bref = jnp.where(rows >= k * blk, bh[k * blk + ref0:k * blk + ref0 + 1, :], bref)
                x = bh - bref
                x = jnp.where(qside, x, -x)
            else:
                fine += 1
                x = expo[fine * c:(fine + 1) * c, sl]
            z = (jnp.where(qside, qh, kh) * jnp.exp2(x)).astype(BF16)
            scores = jnp.where(in_level[li], _dot_nt(z, z), scores)

        lhs = jnp.concatenate([(qh * eb).astype(BF16), scores.astype(BF16)], axis=1)
        rhs = jnp.concatenate([st.T.astype(BF16), vh], axis=0)
        o = _dot(lhs, rhs) + jnp.sum(qh * kh, axis=-1, keepdims=True) * vh.astype(F32)

        khat = (kh * jnp.exp2(btot - bh)).astype(BF16)
        states.append(st * eb[last:last + 1, :] + _dot_tn(vh, khat))

        if final:
            osum = o + ob_ref[0, :, sl]
            ms = jnp.mean(osum * osum, axis=-1, keepdims=True)
            o = (osum * lax.rsqrt(ms + EPS) * gain_ref[:, sl]
                 * jax.nn.silu(g_ref[0, :, sl].astype(F32)))
        outs.append(o.astype(o_ref.dtype))

    o_ref[0] = jnp.concatenate(outs, axis=1)
    st_ref[...] = jnp.stack(states, axis=0)


def _hgrn(p, lb, *, seq, ctx_len, mix, rev, fcol, extra=None):
    b, r, _ = p.shape
    c = HG_CHUNK
    nl, nc = seq // c, ctx_len // c
    nheads = mix // HG_HEAD_DIM

    if rev:
        def order(n):
            return nl + nc - 1 - n
    else:
        def order(n):
            return jnp.where(n < nc, nl + n, n - nc)

    def pspec(col):
        return pl.BlockSpec((1, c, mix), lambda i, n: (i, order(n), col))

    sums, level = _hgrn_tables(c, rev)
    in_specs = [pspec(0), pspec(fcol), pspec(3), pl.BlockSpec((1, mix), lambda i, n: (0, 0)),
                pl.BlockSpec(sums.shape, lambda i, n: (0, 0)),
                pl.BlockSpec(level.shape, lambda i, n: (0, 0))]
    args = [p, p, p, lb, sums, level]
    final = extra is not None
    if final:
        o_other, gain = extra
        in_specs += [pl.BlockSpec((1, c, mix), lambda i, n: (i, order(n), 0)), pspec(4),
                     pl.BlockSpec((1, mix), lambda i, n: (0, 0))]
        args += [o_other, p, gain]
    return pl.pallas_call(
        functools.partial(_hgrn_kernel, rev=rev, chunk=c, nheads=nheads, final=final),
        grid=(b, nl + nc),
        in_specs=in_specs,
        out_specs=pl.BlockSpec((1, c, mix), lambda i, n: (i, order(n), 0)),
        out_shape=jax.ShapeDtypeStruct((b, r, mix), BF16 if final else F32),
        scratch_shapes=[pltpu.VMEM((nheads, HG_HEAD_DIM, HG_HEAD_DIM), F32)],
        compiler_params=_params(("parallel", "arbitrary")),
        name="hgrn_bwd" if rev else "hgrn_fwd",
    )(*args)


def _rope(x, cos, sin):
    lane = lax.broadcasted_iota(jnp.int32, x.shape, 1)
    first = (lane % 32) < 16
    swapped = jnp.where(first, pltpu.roll(x, LANES - 16, 1), pltpu.roll(x, 16, 1))
    return x * cos + swapped * sin


def _attn_kernel(*refs, windowed, nblk, kvh, ctx_len):
    if windowed:
        (q_ref, kp_ref, kc_ref, kn_ref, vp_ref, vc_ref, vn_ref, kx_ref, vx_ref,
         cq_ref, sq_ref, cp_ref, sp_ref, cn_ref, sn_ref, sink_ref, o_ref) = refs
    else:
        q_ref, kx_ref, vx_ref, sink_ref, o_ref = refs
    blk = ATTN_BLOCK
    hd = ATTN_HEAD_DIM
    gw = ATTN_GROUP * hd
    n = pl.program_id(1)
    log2e = 1.4426950408889634
    scale = ATTN_HEAD_DIM ** -0.5 * log2e

    if windowed:
        kwin = [_rope(kp_ref[0].astype(F32), cp_ref[...], sp_ref[...]),
                _rope(kc_ref[0].astype(F32), cq_ref[...], sq_ref[...]),
                _rope(kn_ref[0].astype(F32), cn_ref[...], sn_ref[...])]
        vwin = [vp_ref[0], vc_ref[0], vn_ref[0]]
        nkeys = ctx_len + 3 * blk
        qi = lax.broadcasted_iota(jnp.int32, (blk, nkeys), 0)
        kj = lax.broadcasted_iota(jnp.int32, (blk, nkeys), 1) - ctx_len
        valid = (kj < 0) | ((kj >= qi) & (kj <= qi + 2 * blk)
                            & ((kj >= blk) | (n > 0)) & ((kj < 2 * blk) | (n < nblk - 1)))

    for g in range(kvh):
        ks = slice(g * hd, (g + 1) * hd)
        k_parts = [kx_ref[0, :, ks]]
        v_parts = [vx_ref[0, :, ks]]
        if windowed:
            k_parts += [kw[:, ks] for kw in kwin]
            v_parts += [vw[:, ks] for vw in vwin]
        k_all = jnp.concatenate(k_parts, axis=0).astype(BF16)
        v_all = jnp.concatenate(v_parts, axis=0).astype(BF16)

        q_heads = []
        for c4 in range(gw // LANES):
            col = g * gw + c4 * LANES
            qc = q_ref[0, :, col:col + LANES].astype(F32)
            if windowed:
                qc = _rope(qc, cq_ref[...], sq_ref[...])
            qc = qc * scale
            q_heads += [qc[:, :hd], qc[:, hd:]]
        q_stack = jnp.concatenate(q_heads, axis=0).astype(BF16)
        s_all = _dot_nt(q_stack, k_all)

        p_list, inv_list = [], []
        for r in range(ATTN_GROUP):
            s = s_all[r * blk:(r + 1) * blk, :]
            if windowed:
                s = jnp.where(valid, s, -jnp.inf)
            sink = sink_ref[g * ATTN_GROUP + r] * log2e
            m = jnp.maximum(jnp.max(s, axis=-1, keepdims=True), sink)
            p = jnp.exp2(s - m)
            denom = jnp.sum(p, axis=-1, keepdims=True) + jnp.exp2(sink - m)
            p_list.append(p.astype(BF16))
            inv_list.append(1.0 / denom)
        o_all = _dot(jnp.concatenate(p_list, axis=0), v_all)
        outs = [o_all[r * blk:(r + 1) * blk, :] * inv_list[r] for r in range(ATTN_GROUP)]
        o_ref[0, :, g * gw:(g + 1) * gw] = jnp.concatenate(outs, axis=1).astype(o_ref.dtype)


def _attention(p, cos, sin, sink, *, seq, ctx_len, mix, kvw):
    b, r, _ = p.shape
    blk = ATTN_BLOCK
    nblk = seq // blk
    kvh = kvw // ATTN_HEAD_DIM
    kcol = 6 * mix // kvw
    ctx_blk = seq // ctx_len

    def kv(col, off):
        return pl.BlockSpec((1, blk, kvw),
                            lambda i, n: (i, jnp.clip(n + off, 0, nblk - 1), col))

    def tab(off):
        return pl.BlockSpec((blk, LANES), lambda i, n: (jnp.clip(n + off, 0, nblk - 1), 0))

    smem = pl.BlockSpec(memory_space=pltpu.SMEM)
    ctx_k = pl.BlockSpec((1, ctx_len, kvw), lambda i, n: (i, ctx_blk, kcol))
    ctx_v = pl.BlockSpec((1, ctx_len, kvw), lambda i, n: (i, ctx_blk, kcol + 1))
    out_shape = jax.ShapeDtypeStruct((b, r, mix), BF16)

    lat = pl.pallas_call(
        functools.partial(_attn_window_kernel, windowed=True, nblk=nblk, kvh=kvh, ctx_len=ctx_len),
        grid=(b, r // blk),
        in_specs=[pl.BlockSpec((1, blk, mix), lambda i, n: (i, jnp.minimum(n, nblk - 1), 5)),
                  kv(kcol, -1), kv(kcol, 0), kv(kcol, 1),
                  kv(kcol + 1, -1), kv(kcol + 1, 0), kv(kcol + 1, 1),
                  ctx_k, ctx_v,
                  tab(0), tab(0), tab(-1), tab(-1), tab(1), tab(1), smem],
        out_specs=pl.BlockSpec((1, blk, mix), lambda i, n: (i, n, 0)),
        out_shape=out_shape,
        compiler_params=_params(("parallel", "arbitrary")),
        name="attn_window",
    )(p, p, p, p, p, p, p, p, p, cos, sin, cos, sin, cos, sin, sink)
    return lat, (ctx_k, ctx_v, smem, out_shape, kvh)


def _context_attention(p, attn_lat, sink, aux, *, seq, ctx_len, mix):
    ctx_k, ctx_v, smem, out_shape, kvh = aux
    b = p.shape[0]
    blk = ATTN_BLOCK
    base = seq // blk
    return pl.pallas_call(
        functools.partial(_attn_ctx_kernel, windowed=False, nblk=0, kvh=kvh, ctx_len=ctx_len),
        grid=(b, ctx_len // blk),
        in_specs=[pl.BlockSpec((1, blk, mix), lambda i, n: (i, base + n, 5)),
                  ctx_k, ctx_v, smem,
                  pl.BlockSpec(memory_space=pl.ANY)],
        out_specs=pl.BlockSpec((1, blk, mix), lambda i, n: (i, base + n, 0)),
        out_shape=out_shape,
        input_output_aliases={4: 0},
        compiler_params=_params(("parallel", "arbitrary")),
        name="attn_context",
    )(p, p, p, sink, attn_lat)


def _attn_ctx_kernel(q_ref, kx_ref, vx_ref, sink_ref, prev_ref, o_ref, **kw):
    del prev_ref
    _attn_kernel(q_ref, kx_ref, vx_ref, sink_ref, o_ref, **kw)


def _attn_window_kernel(*refs, nblk, **kw):
    n = pl.program_id(1)
    o_ref = refs[-1]

    @pl.when(n < nblk)
    def _():
        _attn_kernel(*refs, nblk=nblk, **kw)

    @pl.when(n >= nblk)
    def _():
        o_ref[...] = jnp.zeros_like(o_ref)


def _merge_kernel(a_ref, t_ref, ga_ref, gb_ref, wa_ref, wb_ref, o_ref):
    ya = _dot(a_ref[0], wa_ref[...])
    yb = _dot(t_ref[0], wb_ref[...])
    y = (jax.nn.sigmoid(ga_ref[0].astype(F32)) * ya
         + jax.nn.sigmoid(gb_ref[0].astype(F32)) * yb)
    o_ref[0] = y.astype(o_ref.dtype)


def _merge(a, attn, p, wa, wb, *, tm, ntiles, tn, ga_col, gb_col):
    b, r, mix = a.shape
    d = wa.shape[1]
    return pl.pallas_call(
        _merge_kernel,
        grid=(b, ntiles, d // tn),
        in_specs=[pl.BlockSpec((1, tm, mix), lambda i, t, j: (i, t, 0)),
                  pl.BlockSpec((1, tm, mix), lambda i, t, j: (i, t, 0)),
                  pl.BlockSpec((1, tm, tn), lambda i, t, j: (i, t, ga_col + j)),
                  pl.BlockSpec((1, tm, tn), lambda i, t, j: (i, t, gb_col + j)),
                  pl.BlockSpec((mix, tn), lambda i, t, j: (0, j)),
                  pl.BlockSpec((mix, tn), lambda i, t, j: (0, j))],
        out_specs=pl.BlockSpec((1, tm, tn), lambda i, t, j: (i, t, j)),
        out_shape=jax.ShapeDtypeStruct((b, ntiles * tm, d), BF16),
        compiler_params=_params(("parallel", "parallel", "arbitrary")),
        name="branch_merge",
    )(a, attn, p, p, wa, wb)


def _out_proj_kernel(y_ref, w_ref, x_ref, modl_ref, modc_ref, o_ref, *, tm, seq):
    is_ctx = _is_ctx_rows(pl.program_id(1) * tm, tm, seq)
    gate = jnp.where(is_ctx, modc_ref[0, 2:3, :], modl_ref[0, 2:3, :])
    o_ref[0] = x_ref[0] + gate * _dot(y_ref[0], w_ref[...])


def _out_proj(y, w, x, mods, *, seq, tm, ntiles, tn):
    b, r, d = x.shape
    nb = mods.shape[0] - 1
    return pl.pallas_call(
        functools.partial(_out_proj_kernel, tm=tm, seq=seq),
        grid=(b, ntiles, d // tn),
        in_specs=[pl.BlockSpec((1, tm, d), lambda i, t, j: (i, t, 0)),
                  pl.BlockSpec((d, tn), lambda i, t, j: (0, j)),
                  pl.BlockSpec((1, tm, tn), lambda i, t, j: (i, t, j)),
                  pl.BlockSpec((1, 6, tn), lambda i, t, j: (i, 0, j)),
                  pl.BlockSpec((1, 6, tn), lambda i, t, j: (nb, 0, j))],
        out_specs=pl.BlockSpec((1, tm, tn), lambda i, t, j: (i, t, j)),
        out_shape=jax.ShapeDtypeStruct((b, ntiles * tm, d), F32),
        compiler_params=_params(("parallel", "parallel", "arbitrary")),
        name="out_proj",
    )(y, w, x, mods, mods)


def _ffn_kernel(x_ref, modl_ref, modc_ref, g_ref, wg_ref, wu_ref, wd_ref, o_ref, h_ref, acc_ref,
                *, tm, seq):
    j = pl.program_id(2)
    last = pl.num_programs(2) - 1
    tile_row0 = pl.program_id(1) * tm

    @pl.when(j == 0)
    def _():
        _norm_mod_rows(x_ref, h_ref, g_ref, modl_ref, modc_ref, 3, tile_row0, tm, seq)
        acc_ref[...] = jnp.zeros_like(acc_ref)

    h = h_ref[...]
    act = jax.nn.silu(_dot(h, wg_ref[...])) * _dot(h, wu_ref[...])
    acc_ref[...] += _dot(act.astype(BF16), wd_ref[...])

    @pl.when(j == last)
    def _():
        def body(r0, rc):
            rows = pl.ds(r0, rc)
            is_ctx = _is_ctx_rows(tile_row0 + r0, rc, seq)
            o_ref[0, rows, :] = (x_ref[0, rows, :]
                                 + _mod_row(modl_ref, modc_ref, 5, is_ctx) * acc_ref[rows, :])

        _for_row_chunks(tm, body)


def _ffn(x, mods, gain, wg, wu, wd, *, seq, tm, tf):
    b, r, d = x.shape
    f = wg.shape[1]
    nb = mods.shape[0] - 1
    return pl.pallas_call(
        functools.partial(_ffn_kernel, tm=tm, seq=seq),
        grid=(b, r // tm, f // tf),
        in_specs=[pl.BlockSpec((1, tm, d), lambda i, t, j: (i, t, 0)),
                  pl.BlockSpec((1, 6, d), lambda i, t, j: (i, 0, 0)),
                  pl.BlockSpec((1, 6, d), lambda i, t, j: (nb, 0, 0)),
                  pl.BlockSpec((1, d), lambda i, t, j: (0, 0)),
                  pl.BlockSpec((d, tf), lambda i, t, j: (0, j)),
                  pl.BlockSpec((d, tf), lambda i, t, j: (0, j)),
                  pl.BlockSpec((tf, d), lambda i, t, j: (j, 0))],
        out_specs=pl.BlockSpec((1, tm, d), lambda i, t, j: (i, t, 0)),
        out_shape=jax.ShapeDtypeStruct((b, r, d), F32),
        scratch_shapes=[pltpu.VMEM((tm, d), BF16), pltpu.VMEM((tm, d), F32)],
        compiler_params=_params(("parallel", "parallel", "arbitrary")),
        name="ffn_dense",
    )(x, mods, mods, gain, wg, wu, wd)


def _route_kernel(x_ref, modl_ref, g_ref, wr_ref, h_ref, r_ref, *, n_experts, tm):
    def body(r0, rc):
        rows = pl.ds(r0, rc)
        h = _norm_mod(x_ref[0, rows, :], g_ref[...], modl_ref[0, 3:4, :], modl_ref[0, 4:5, :])
        h_ref[rows, :] = h

    _for_row_chunks(tm, body)

    logits = jnp.dot(h_ref[...], wr_ref[...], preferred_element_type=F32,
                     precision=lax.Precision.HIGHEST)
    lane = lax.broadcasted_iota(jnp.int32, logits.shape, 1)
    lg = jnp.where(lane < n_experts, logits, -jnp.inf)
    m1 = jnp.max(lg, axis=-1, keepdims=True)
    i1 = jnp.min(jnp.where(lg == m1, lane, LANES), axis=-1, keepdims=True)
    lg2 = jnp.where(lane == i1, -jnp.inf, lg)
    m2 = jnp.max(lg2, axis=-1, keepdims=True)
    i2 = jnp.min(jnp.where(lg2 == m2, lane, LANES), axis=-1, keepdims=True)
    e2 = jnp.exp(m2 - m1)
    w1 = 1.0 / (1.0 + e2)
    w2 = e2 / (1.0 + e2)
    r_ref[...] = jnp.where(lane == 0, i1.astype(F32),
                           jnp.where(lane == 1, i2.astype(F32),
                                     jnp.where(lane == 2, w1, jnp.where(lane == 3, w2, 0.0))))


def _route(x, mods, gain, wr, *, seq, tm, n_experts):
    b, r, d = x.shape
    nt = seq // tm
    return pl.pallas_call(
        functools.partial(_route_kernel, n_experts=n_experts, tm=tm),
        grid=(b, nt),
        in_specs=[pl.BlockSpec((1, tm, d), lambda i, t: (i, t, 0)),
                  pl.BlockSpec((1, 6, d), lambda i, t: (i, 0, 0)),
                  pl.BlockSpec((1, d), lambda i, t: (0, 0)),
                  pl.BlockSpec((d, LANES), lambda i, t: (0, 0))],
        out_specs=[pl.BlockSpec((tm, d), lambda i, t: (i * nt + t, 0)),
                   pl.BlockSpec((tm, LANES), lambda i, t: (i * nt + t, 0))],
        out_shape=[jax.ShapeDtypeStruct((b * seq, d), F32),
                   jax.ShapeDtypeStruct((b * seq, LANES), F32)],
        compiler_params=_params(("parallel", "parallel")),
        name="moe_route",
    )(x, mods, gain, wr)


def _moe_ffn_kernel(te_ref, nu_ref, idx0_ref, idxn_ref, h_hbm, wg_ref, wu_ref, wd_ref, o_ref,
                    xbuf, hb_ref, sem, *, tm, nsteps):
    del te_ref
    i, j = pl.program_id(0), pl.program_id(1)
    nu = nu_ref[0]
    used = i < nu
    slot = i % 2
    per = tm // nsteps // 8 * 8
    rem = tm - per * nsteps

    def row_copy(idx_ref, s, r):
        return pltpu.make_async_copy(h_hbm.at[pl.ds(idx_ref[0, 0, r], 1), :],
                                     xbuf.at[s, pl.ds(r, 1), :], sem.at[s])

    def issue(idx_ref, s, lo, n):
        def body(k, carry):
            row_copy(idx_ref, s, lo + k).start()
            return carry

        lax.fori_loop(0, n, body, 0, unroll=8)

    @pl.when((i == 0) & (j == 0))
    def _():
        issue(idx0_ref, 0, 0, tm)

    @pl.when(used & (j == 0))
    def _():
        pltpu.make_async_copy(h_hbm.at[pl.ds(0, tm), :], xbuf.at[slot], sem.at[slot]).wait()

        def cast(r0, rc):
            hb_ref[pl.ds(r0, rc), :] = xbuf[slot, pl.ds(r0, rc), :].astype(BF16)

        _for_row_chunks(tm, cast)

    @pl.when(i + 1 < nu)
    def _():
        issue(idxn_ref, 1 - slot, j * per, per)

    if rem:
        @pl.when((i + 1 < nu) & (j == 0))
        def _():
            issue(idxn_ref, 1 - slot, nsteps * per, rem)

    @pl.when(j == 0)
    def _():
        o_ref[...] = jnp.zeros_like(o_ref)

    @pl.when(used)
    def _():
        h = hb_ref[...]
        act = jax.nn.silu(_dot(h, wg_ref[0])) * _dot(h, wu_ref[0])
        o_ref[...] += _dot(act.astype(BF16), wd_ref[0])


def _moe_ffn(h, src, tile_expert, n_used, wg, wu, wd, *, tm, tf):
    ns = src.shape[0]
    d = h.shape[1]
    f = wg.shape[2]
    nt = ns // tm

    def w_up(i, j, te, nu):
        return (te[i], 0, jnp.where(i < nu[0], j, 0))

    def w_down(i, j, te, nu):
        return (te[i], jnp.where(i < nu[0], j, 0), 0)

    smem = pltpu.SMEM
    grid_spec = pltpu.PrefetchScalarGridSpec(
        num_scalar_prefetch=2,
        grid=(nt, f // tf),
        in_specs=[pl.BlockSpec((1, 1, tm), lambda i, j, te, nu: (0, 0, 0), memory_space=smem),
                  pl.BlockSpec((1, 1, tm), lambda i, j, te, nu: (jnp.minimum(i + 1, nt - 1), 0, 0),
                               memory_space=smem),
                  pl.BlockSpec(memory_space=pl.ANY),
                  pl.BlockSpec((1, d, tf), w_up),
                  pl.BlockSpec((1, d, tf), w_up),
                  pl.BlockSpec((1, tf, d), w_down)],
        out_specs=pl.BlockSpec((tm, d), lambda i, j, te, nu: (i, 0)),
        scratch_shapes=[pltpu.VMEM((2, tm, d), F32), pltpu.VMEM((tm, d), BF16),
                        pltpu.SemaphoreType.DMA((2,))])
    src3 = src.reshape(nt, 1, tm)
    return pl.pallas_call(
        functools.partial(_moe_ffn_kernel, tm=tm, nsteps=f // tf),
        grid_spec=grid_spec,
        out_shape=jax.ShapeDtypeStruct((ns, d), F32),
        compiler_params=_params(("arbitrary", "arbitrary")),
        name="moe_ffn",
    )(tile_expert, n_used, src3, src3, h, wg, wu, wd)


def _combine_kernel(s0_ref, s1_ref, x_ref, modl_ref, r_ref, fn_ref, y_ref, o_ref, ya_ref, yb_ref, sem,
                    *, cm):
    def issue(r, carry):
        pltpu.make_async_copy(y_ref.at[pl.ds(s0_ref[0, 0, r], 1), :],
                              ya_ref.at[pl.ds(r, 1), :], sem.at[0]).start()
        pltpu.make_async_copy(y_ref.at[pl.ds(s1_ref[0, 0, r], 1), :],
                              yb_ref.at[pl.ds(r, 1), :], sem.at[1]).start()
        return carry

    lax.fori_loop(0, cm, issue, 0)

    def drain(r, carry):
        pltpu.make_async_copy(y_ref.at[pl.ds(0, 1), :], ya_ref.at[pl.ds(r, 1), :], sem.at[0]).wait()
        pltpu.make_async_copy(y_ref.at[pl.ds(0, 1), :], yb_ref.at[pl.ds(r, 1), :], sem.at[1]).wait()
        return carry

    lax.fori_loop(0, cm, drain, 0)

    route = r_ref[...]
    moe = route[:, 2:3] * ya_ref[...] + route[:, 3:4] * yb_ref[...]
    xn = x_ref[0] + modl_ref[0, 5:6, :] * moe
    ms = jnp.mean(xn * xn, axis=-1, keepdims=True)
    o_ref[0] = xn * lax.rsqrt(ms + EPS) * fn_ref[...]


def _combine(x, mods, route, final_gain, y, slot0, slot1, *, seq, cm):
    b, r, d = x.shape
    nt = seq // cm

    def sidx():
        return pl.BlockSpec((1, 1, cm), lambda i, t: (i * nt + t, 0, 0), memory_space=pltpu.SMEM)

    return pl.pallas_call(
        functools.partial(_combine_kernel, cm=cm),
        grid=(b, nt),
        in_specs=[sidx(), sidx(),
                  pl.BlockSpec((1, cm, d), lambda i, t: (i, t, 0)),
                  pl.BlockSpec((1, 6, d), lambda i, t: (i, 0, 0)),
                  pl.BlockSpec((cm, LANES), lambda i, t: (i * nt + t, 0)),
                  pl.BlockSpec((1, d), lambda i, t: (0, 0)),
                  pl.BlockSpec(memory_space=pl.ANY)],
        out_specs=pl.BlockSpec((1, cm, d), lambda i, t: (i, t, 0)),
        out_shape=jax.ShapeDtypeStruct((b, seq, d), F32),
        scratch_shapes=[pltpu.VMEM((cm, d), F32), pltpu.VMEM((cm, d), F32),
                        pltpu.SemaphoreType.DMA((2,))],
        compiler_params=_params(("arbitrary", "arbitrary")),
        name="moe_combine",
    )(slot0.reshape(b * nt, 1, cm), slot1.reshape(b * nt, 1, cm), x, mods, route, final_gain, y)


def _moe_plan(route, n_experts, tm):
    t = route.shape[0]
    eid = route[:, :TOP_K].astype(jnp.int32).reshape(-1)
    onehot = (eid[:, None] == jnp.arange(n_experts, dtype=jnp.int32)[None, :]).astype(jnp.int32)
    csum = jnp.cumsum(onehot, axis=0)
    rank = jnp.sum((csum - onehot) * onehot, axis=1)
    counts = csum[-1]
    padded = ((counts + tm - 1) // tm) * tm
    ends = jnp.cumsum(padded)
    starts = ends - padded
    slot = starts[eid] + rank
    ns = TOP_K * t + n_experts * tm
    token = jnp.arange(TOP_K * t, dtype=jnp.int32) // TOP_K
    src = jnp.zeros((ns,), jnp.int32).at[slot].set(token)
    tile_start = jnp.arange(ns // tm, dtype=jnp.int32) * tm
    tile_expert = jnp.minimum(jnp.sum((tile_start[:, None] >= ends[None, :]).astype(jnp.int32), axis=1),
                              n_experts - 1)
    n_used = (ends[-1] // tm).astype(jnp.int32).reshape(1)
    slots = slot.reshape(t, TOP_K)
    return src, tile_expert, n_used, slots[:, 0], slots[:, 1]


def _rope_tables(length):
    rows = length // GRID_W
    row = jnp.repeat(jnp.arange(rows, dtype=F32), GRID_W)
    col = jnp.tile(jnp.arange(GRID_W, dtype=F32), rows)
    n_freq = ATTN_HEAD_DIM // 4
    inv = ROPE_THETA ** (-jnp.arange(n_freq, dtype=F32) / n_freq)
    ar, ac = row[:, None] * inv, col[:, None] * inv
    cos = jnp.concatenate([jnp.cos(ar), jnp.cos(ar), jnp.cos(ac), jnp.cos(ac)], axis=1)
    sin = jnp.concatenate([-jnp.sin(ar), jnp.sin(ar), -jnp.sin(ac), jnp.sin(ac)], axis=1)
    reps = LANES // ATTN_HEAD_DIM
    return jnp.tile(cos, (1, reps)), jnp.tile(sin, (1, reps))


def _lower_bound(raw, layer):
    p = jax.nn.softmax(raw.astype(F32), axis=0)
    return (jnp.cumsum(p, axis=0) - p[0])[layer]


def kernel(x, c, ctx, c_ctx, w_mod, b_mod, norm_mix, norm_ffn, w_in, hg_lb_fwd, hg_lb_bwd, hg_norm,
           attn_sink, w_branch_a, w_branch_b, w_out, ffn_w_gate, ffn_w_up, ffn_w_down,
           moe_router, moe_w_gate, moe_w_up, moe_w_down, final_norm):
    b, seq, d = x.shape
    ctx_len = ctx.shape[1]
    depth = w_mod.shape[0]
    r = seq + ctx_len
    mix = d // 2
    kvw = (mix // ATTN_HEAD_DIM // ATTN_GROUP) * ATTN_HEAD_DIM
    n_experts = moe_router.shape[-1]
    assert depth == 2 and kvw % LANES == 0 and seq % ctx_len == 0
    assert seq % HG_CHUNK == 0 and ctx_len % HG_CHUNK == 0 and seq % GRID_W == 0

    tn = 512
    ga = -(-(6 * mix + 2 * kvw) // tn) * tn
    gb = ga + d
    n_in = 6 * mix + 2 * kvw
    tn_in = _pick_tile(gb + d, 768, 2 * LANES)
    tm_full = _pick_tile(r, 1152)
    tm_lat = _pick_tile(seq, 1024)
    tm_ffn = _pick_tile(r, 768)

    mb = -(-(b + 1) // 8) * 8
    cvec = jnp.zeros((mb, d), F32).at[:b].set(c).at[b].set(c_ctx)
    mods_all = _modulation(cvec, w_mod, b_mod)[:, :b + 1].reshape(depth, b + 1, 6, d)

    cos, sin = _rope_tables(seq)
    xs = jnp.concatenate([x, ctx], axis=1)

    for layer in range(depth):
        need_ctx = layer < depth - 1
        mods = mods_all[layer]
        w = w_in[layer]
        wp = jnp.concatenate([w[:, :n_in], jnp.zeros((d, ga - n_in), w.dtype), w[:, n_in:]],
                             axis=1).astype(BF16)
        p = _in_proj(xs, mods, norm_mix[layer].reshape(1, d), wp, seq=seq, tm=tm_full, tn=tn_in)

        lb_f = _lower_bound(hg_lb_fwd, layer).reshape(1, mix)
        lb_b = _lower_bound(hg_lb_bwd, layer).reshape(1, mix)
        o_b = _hgrn(p, lb_b, seq=seq, ctx_len=ctx_len, mix=mix, rev=True, fcol=2)
        a = _hgrn(p, lb_f, seq=seq, ctx_len=ctx_len, mix=mix, rev=False, fcol=1,
                  extra=(o_b, hg_norm[layer].reshape(1, mix)))

        attn, aux = _attention(p, cos, sin, attn_sink[layer], seq=seq, ctx_len=ctx_len, mix=mix, kvw=kvw)
        if need_ctx:
            attn = _context_attention(p, attn, attn_sink[layer], aux, seq=seq, ctx_len=ctx_len, mix=mix)

        tm, nt = (tm_full, r // tm_full) if need_ctx else (tm_lat, seq // tm_lat)
        y = _merge(a, attn, p, w_branch_a[layer].astype(BF16), w_branch_b[layer].astype(BF16),
                   tm=tm, ntiles=nt, tn=tn, ga_col=ga // tn, gb_col=gb // tn)
        xs = _out_proj(y, w_out[layer].astype(BF16), xs, mods, seq=seq, tm=tm, ntiles=nt, tn=tn)

        i = layer // 2
        gain = norm_ffn[layer].reshape(1, d)
        if layer % 2 == 0:
            tf = _pick_tile(ffn_w_gate.shape[-1], 512, LANES)
            xs = _ffn(xs, mods, gain, ffn_w_gate[i].astype(BF16), ffn_w_up[i].astype(BF16),
                      ffn_w_down[i].astype(BF16), seq=seq, tm=tm_ffn, tf=tf)
        else:
            tme = _pick_tile(seq, 512)
            wr = jnp.zeros((d, LANES), F32).at[:, :n_experts].set(moe_router[i])
            h, route = _route(xs, mods, gain, wr, seq=seq, tm=tme, n_experts=n_experts)
            src, tile_expert, n_used, slot0, slot1 = _moe_plan(route, n_experts, tme)
            tf = _pick_tile(moe_w_gate.shape[-1], 1024, LANES)
            yg = _moe_ffn(h, src, tile_expert, n_used, moe_w_gate[i].astype(BF16),
                          moe_w_up[i].astype(BF16), moe_w_down[i].astype(BF16), tm=tme, tf=tf)
            out = _combine(xs, mods, route, final_norm.reshape(1, d), yg, slot0, slot1,
                           seq=seq, cm=_pick_tile(seq, 256))
    return out
```

```python
import functools

import jax
import jax.numpy as jnp
import numpy as np
from jax import lax
from jax.experimental import pallas as pl
from jax.experimental.pallas import tpu as pltpu

HG_HEAD_DIM = 128
ATTN_HEAD_DIM = 64
ATTN_GROUP = 8
ATTN_BLOCK = 128
GRID_W = 64
ROPE_THETA = 10000.0
EPS = 1e-6
TOP_K = 2
LANES = 128
HG_CHUNK = 128
HG_FINE = 8
VMEM_LIMIT = 56 * 1024 * 1024

F32 = jnp.float32
BF16 = jnp.bfloat16


def _pick_tile(n, cap, mult=16):
    best = None
    for t in range(mult, min(cap, n) + 1, mult):
        if n % t == 0:
            best = t
    assert best is not None, (n, cap)
    return best


def _params(sem, vmem=VMEM_LIMIT, **kw):
    return pltpu.CompilerParams(dimension_semantics=sem, vmem_limit_bytes=vmem, **kw)


def _dot(a, b):
    return jnp.dot(a, b, preferred_element_type=F32)


def _dot_nt(a, b):
    return lax.dot_general(a, b, (((1,), (1,)), ((), ())), preferred_element_type=F32)


def _dot_tn(a, b):
    return lax.dot_general(a, b, (((0,), (0,)), ((), ())), preferred_element_type=F32)


def _norm_mod(x, gain, shift, scale):
    ms = jnp.mean(x * x, axis=-1, keepdims=True)
    return x * lax.rsqrt(ms + EPS) * gain * (1.0 + scale) + shift


def _is_ctx_rows(row0, n, seq):
    rows = row0 + lax.broadcasted_iota(jnp.int32, (n, 1), 0)
    return rows >= seq


def _mod_row(modl_ref, modc_ref, k, is_ctx):
    return jnp.where(is_ctx, modc_ref[0, k:k + 1, :], modl_ref[0, k:k + 1, :])


def _for_row_chunks(tm, body):
    rc = _pick_tile(tm, 32)

    def step(k, carry):
        body(pl.multiple_of(k * rc, rc), rc)
        return carry

    lax.fori_loop(0, tm // rc, step, 0)


def _norm_mod_rows(x_ref, h_ref, g_ref, modl_ref, modc_ref, k_shift, tile_row0, tm, seq):
    def body(r0, rc):
        rows = pl.ds(r0, rc)
        is_ctx = _is_ctx_rows(tile_row0 + r0, rc, seq)
        h = _norm_mod(x_ref[0, rows, :], g_ref[...], _mod_row(modl_ref, modc_ref, k_shift, is_ctx),
                      _mod_row(modl_ref, modc_ref, k_shift + 1, is_ctx))
        h_ref[rows, :] = h.astype(h_ref.dtype)

    _for_row_chunks(tm, body)


def _mod_kernel(c_ref, w_ref, b_ref, o_ref):
    act = jax.nn.silu(c_ref[...])
    o_ref[0] = jnp.dot(act, w_ref[0], preferred_element_type=F32,
                       precision=lax.Precision.HIGHEST) + b_ref[0]


def _modulation(cvec, w_mod, b_mod):
    depth, d, n6 = w_mod.shape
    mb = cvec.shape[0]
    tn = _pick_tile(n6, 1024, LANES)
    return pl.pallas_call(
        _mod_kernel,
        grid=(depth, n6 // tn),
        in_specs=[pl.BlockSpec((mb, d), lambda l, j: (0, 0)),
                  pl.BlockSpec((1, d, tn), lambda l, j: (l, 0, j)),
                  pl.BlockSpec((1, 1, tn), lambda l, j: (l, 0, j))],
        out_specs=pl.BlockSpec((1, mb, tn), lambda l, j: (l, 0, j)),
        out_shape=jax.ShapeDtypeStruct((depth, mb, n6), F32),
        compiler_params=_params(("parallel", "arbitrary")),
        name="modulation",
    )(cvec, w_mod, b_mod.reshape(depth, 1, n6))


def _in_proj_kernel(x_ref, modl_ref, modc_ref, g_ref, w_ref, o_ref, h_ref, *, tm, seq):
    tile_row0 = pl.program_id(1) * tm

    @pl.when(pl.program_id(2) == 0)
    def _():
        _norm_mod_rows(x_ref, h_ref, g_ref, modl_ref, modc_ref, 0, tile_row0, tm, seq)

    o_ref[0] = _dot(h_ref[...], w_ref[...]).astype(o_ref.dtype)


def _in_proj(x, mods, gain, w, *, seq, tm, tn):
    b, r, d = x.shape
    npad = w.shape[1]
    nb = mods.shape[0] - 1
    return pl.pallas_call(
        functools.partial(_in_proj_kernel, tm=tm, seq=seq),
        grid=(b, r // tm, npad // tn),
        in_specs=[pl.BlockSpec((1, tm, d), lambda i, t, j: (i, t, 0)),
                  pl.BlockSpec((1, 6, d), lambda i, t, j: (i, 0, 0)),
                  pl.BlockSpec((1, 6, d), lambda i, t, j: (nb, 0, 0)),
                  pl.BlockSpec((1, d), lambda i, t, j: (0, 0)),
                  pl.BlockSpec((d, tn), lambda i, t, j: (0, j))],
        out_specs=pl.BlockSpec((1, tm, tn), lambda i, t, j: (i, t, j)),
        out_shape=jax.ShapeDtypeStruct((b, r, npad), BF16),
        scratch_shapes=[pltpu.VMEM((tm, d), BF16)],
        compiler_params=_params(("parallel", "parallel", "arbitrary")),
        name="in_proj",
    )(x, mods, mods, gain, w)


def _hgrn_tables(c, rev):
    nlev = c.bit_length() - 1
    t = np.arange(c)[:, None]
    u = np.arange(c)[None, :]
    fams = [u <= t]
    level = np.full((c, c), -1)
    for li in range(nlev):
        half = c >> (li + 1)
        blk = 2 * half
        start = (t // blk) * blk
        qside = (t % blk) >= half
        if half <= HG_FINE:
            fams.append(np.where(qside, (u >= start + half) & (u <= t), (u > t) & (u < start + half)))
        level = np.where((t // blk == u // blk) & qside & ((u % blk) < half), li, level)
    sums = np.concatenate(fams, axis=0).astype(np.float32)
    if rev:
        sums = sums.reshape(len(fams), c, c)[:, ::-1, ::-1].reshape(len(fams) * c, c)
        level = level[::-1, ::-1]
    return (jnp.asarray(np.tile(sums, (1, 2)), BF16), jnp.asarray(np.ascontiguousarray(level), jnp.int32))


def _hgrn_kernel(*refs, rev, chunk, nheads, final):
    if final:
        q_ref, f_ref, v_ref, lb_ref, sums_ref, lvl_ref, ob_ref, g_ref, gain_ref, o_ref, st_ref = refs
    else:
        q_ref, f_ref, v_ref, lb_ref, sums_ref, lvl_ref, o_ref, st_ref = refs
    c = chunk
    hd = HG_HEAD_DIM

    @pl.when(pl.program_id(1) == 0)
    def _():
        st_ref[...] = jnp.zeros_like(st_ref)

    lb = lb_ref[...]
    f = lb + (1.0 - lb) * jax.nn.sigmoid(f_ref[0].astype(F32))
    g2 = jnp.log2(f)
    g_hi = g2.astype(BF16)
    g_lo = (g2 - g_hi.astype(F32)).astype(BF16)
    expo = _dot(sums_ref[...], jnp.concatenate([g_hi, g_lo], axis=0))

    rows = lax.broadcasted_iota(jnp.int32, (c, 1), 0)
    level = lvl_ref[...]
    scale = HG_HEAD_DIM ** -0.5
    last = 0 if rev else c - 1
    nlev = c.bit_length() - 1
    in_level = [level == li for li in range(nlev)]
    outs, states = [], []

    for h in range(nheads):
        sl = slice(h * hd, (h + 1) * hd)
        qh = q_ref[0, :, sl].astype(F32) * scale
        kh = 1.0 - f[:, sl]
        vh = v_ref[0, :, sl]
        bh = expo[0:c, sl]
        btot = bh[last:last + 1, :]
        eb = jnp.exp2(bh)
        st = st_ref[h]

        scores = jnp.zeros((c, c), F32)
        fine = 0
        for li in range(nlev):
            half = c >> (li + 1)
            blk = 2 * half
            qside = ((rows & half) == 0) if rev else ((rows & half) != 0)
            if half > HG_FINE:
                ref0 = half if rev else half - 1
                bref = bh[ref0:ref0 + 1, :]
                for k in range(1, c // blk):
                    bref = jnp.where(rows >= k * blk, bh[k * blk + ref0:k * blk + ref0 + 1, :], bref)
                x = bh - bref
                x = jnp.where(qside, x, -x)
            else:
                fine += 1
                x = expo[fine * c:(fine + 1) * c, sl]
            z = (jnp.where(qside, qh, kh) * jnp.exp2(x)).astype(BF16)
            scores = jnp.where(in_level[li], _dot_nt(z, z), scores)

        lhs = jnp.concatenate([(qh * eb).astype(BF16), scores.astype(BF16)], axis=1)
        rhs = jnp.concatenate([st.T.astype(BF16), vh], axis=0)
        o = _dot(lhs, rhs) + jnp.sum(qh * kh, axis=-1, keepdims=True) * vh.astype(F32)

        khat = (kh * jnp.exp2(btot - bh)).astype(BF16)
        states.append(st * eb[last:last + 1, :] + _dot_tn(vh, khat))

        if final:
            osum = o + ob_ref[0, :, sl]
            ms = jnp.mean(osum * osum, axis=-1, keepdims=True)
            o = (osum * lax.rsqrt(ms + EPS) * gain_ref[:, sl]
                 * jax.nn.silu(g_ref[0, :, sl].astype(F32)))
        outs.append(o.astype(o_ref.dtype))

    o_ref[0] = jnp.concatenate(outs, axis=1)
    st_ref[...] = jnp.stack(states, axis=0)


def _hgrn(p, lb, *, seq, ctx_len, mix, rev, fcol, extra=None):
    b, r, _ = p.shape
    c = HG_CHUNK
    nl, nc = seq // c, ctx_len // c
    nheads = mix // HG_HEAD_DIM

    if rev:
        def order(n):
            return nl + nc - 1 - n
    else:
        def order(n):
            return jnp.where(n < nc, nl + n, n - nc)

    def pspec(col):
        return pl.BlockSpec((1, c, mix), lambda i, n: (i, order(n), col))

    sums, level = _hgrn_tables(c, rev)
    in_specs = [pspec(0), pspec(fcol), pspec(3), pl.BlockSpec((1, mix), lambda i, n: (0, 0)),
                pl.BlockSpec(sums.shape, lambda i, n: (0, 0)),
                pl.BlockSpec(level.shape, lambda i, n: (0, 0))]
    args = [p, p, p, lb, sums, level]
    final = extra is not None
    if final:
        o_other, gain = extra
        in_specs += [pl.BlockSpec((1, c, mix), lambda i, n: (i, order(n), 0)), pspec(4),
                     pl.BlockSpec((1, mix), lambda i, n: (0, 0))]
        args += [o_other, p, gain]
    return pl.pallas_call(
        functools.partial(_hgrn_kernel, rev=rev, chunk=c, nheads=nheads, final=final),
        grid=(b, nl + nc),
        in_specs=in_specs,
        out_specs=pl.BlockSpec((1, c, mix), lambda i, n: (i, order(n), 0)),
        out_shape=jax.ShapeDtypeStruct((b, r, mix), BF16 if final else F32),
        scratch_shapes=[pltpu.VMEM((nheads, HG_HEAD_DIM, HG_HEAD_DIM), F32)],
        compiler_params=_params(("parallel", "arbitrary")),
        name="hgrn_bwd" if rev else "hgrn_fwd",
    )(*args)


def _rope(x, cos, sin):
    lane = lax.broadcasted_iota(jnp.int32, x.shape, 1)
    first = (lane % 32) < 16
    swapped = jnp.where(first, pltpu.roll(x, LANES - 16, 1), pltpu.roll(x, 16, 1))
    return x * cos + swapped * sin


def _attn_kernel(*refs, windowed, nblk, kvh, ctx_len):
    if windowed:
        (q_ref, kp_ref, kc_ref, kn_ref, vp_ref, vc_ref, vn_ref, kx_ref, vx_ref,
         cq_ref, sq_ref, cp_ref, sp_ref, cn_ref, sn_ref, sink_ref, o_ref) = refs
    else:
        q_ref, kx_ref, vx_ref, sink_ref, o_ref = refs
    blk = ATTN_BLOCK
    hd = ATTN_HEAD_DIM
    gw = ATTN_GROUP * hd
    n = pl.program_id(1)
    log2e = 1.4426950408889634
    scale = ATTN_HEAD_DIM ** -0.5 * log2e

    if windowed:
        kwin = [_rope(kp_ref[0].astype(F32), cp_ref[...], sp_ref[...]),
                _rope(kc_ref[0].astype(F32), cq_ref[...], sq_ref[...]),
                _rope(kn_ref[0].astype(F32), cn_ref[...], sn_ref[...])]
        vwin = [vp_ref[0], vc_ref[0], vn_ref[0]]
        nkeys = ctx_len + 3 * blk
        qi = lax.broadcasted_iota(jnp.int32, (blk, nkeys), 0)
        kj = lax.broadcasted_iota(jnp.int32, (blk, nkeys), 1) - ctx_len
        valid = (kj < 0) | ((kj >= qi) & (kj <= qi + 2 * blk)
                            & ((kj >= blk) | (n > 0)) & ((kj < 2 * blk) | (n < nblk - 1)))

    for g in range(kvh):
        ks = slice(g * hd, (g + 1) * hd)
        k_parts = [kx_ref[0, :, ks]]
        v_parts = [vx_ref[0, :, ks]]
        if windowed:
            k_parts += [kw[:, ks] for kw in kwin]
            v_parts += [vw[:, ks] for vw in vwin]
        k_all = jnp.concatenate(k_parts, axis=0).astype(BF16)
        v_all = jnp.concatenate(v_parts, axis=0).astype(BF16)

        q_heads = []
        for c4 in range(gw // LANES):
            col = g * gw + c4 * LANES
            qc = q_ref[0, :, col:col + LANES].astype(F32)
            if windowed:
                qc = _rope(qc, cq_ref[...], sq_ref[...])
            qc = qc * scale
            q_heads += [qc[:, :hd], qc[:, hd:]]
        q_stack = jnp.concatenate(q_heads, axis=0).astype(BF16)
        s_all = _dot_nt(q_stack, k_all)

        p_list, inv_list = [], []
        for r in range(ATTN_GROUP):
            s = s_all[r * blk:(r + 1) * blk, :]
            if windowed:
                s = jnp.where(valid, s, -jnp.inf)
            sink = sink_ref[g * ATTN_GROUP + r] * log2e
            m = jnp.maximum(jnp.max(s, axis=-1, keepdims=True), sink)
            p = jnp.exp2(s - m)
            denom = jnp.sum(p, axis=-1, keepdims=True) + jnp.exp2(sink - m)
            p_list.append(p.astype(BF16))
            inv_list.append(1.0 / denom)
        o_all = _dot(jnp.concatenate(p_list, axis=0), v_all)
        outs = [o_all[r * blk:(r + 1) * blk, :] * inv_list[r] for r in range(ATTN_GROUP)]
        o_ref[0, :, g * gw:(g + 1) * gw] = jnp.concatenate(outs, axis=1).astype(o_ref.dtype)


def _attention(p, cos, sin, sink, *, seq, ctx_len, mix, kvw):
    b, r, _ = p.shape
    blk = ATTN_BLOCK
    nblk = seq // blk
    kvh = kvw // ATTN_HEAD_DIM
    kcol = 6 * mix // kvw
    ctx_blk = seq // ctx_len

    def kv(col, off):
        return pl.BlockSpec((1, blk, kvw),
                            lambda i, n: (i, jnp.clip(n + off, 0, nblk - 1), col))

    def tab(off):
        return pl.BlockSpec((blk, LANES), lambda i, n: (jnp.clip(n + off, 0, nblk - 1), 0))

    smem = pl.BlockSpec(memory_space=pltpu.SMEM)
    ctx_k = pl.BlockSpec((1, ctx_len, kvw), lambda i, n: (i, ctx_blk, kcol))
    ctx_v = pl.BlockSpec((1, ctx_len, kvw), lambda i, n: (i, ctx_blk, kcol + 1))
    out_shape = jax.ShapeDtypeStruct((b, r, mix), BF16)

    lat = pl.pallas_call(
        functools.partial(_attn_window_kernel, windowed=True, nblk=nblk, kvh=kvh, ctx_len=ctx_len),
        grid=(b, r // blk),
        in_specs=[pl.BlockSpec((1, blk, mix), lambda i, n: (i, jnp.minimum(n, nblk - 1), 5)),
                  kv(kcol, -1), kv(kcol, 0), kv(kcol, 1),
                  kv(kcol + 1, -1), kv(kcol + 1, 0), kv(kcol + 1, 1),
                  ctx_k, ctx_v,
                  tab(0), tab(0), tab(-1), tab(-1), tab(1), tab(1), smem],
        out_specs=pl.BlockSpec((1, blk, mix), lambda i, n: (i, n, 0)),
        out_shape=out_shape,
        compiler_params=_params(("parallel", "arbitrary")),
        name="attn_window",
    )(p, p, p, p, p, p, p, p, p, cos, sin, cos, sin, cos, sin, sink)
    return lat, (ctx_k, ctx_v, smem, out_shape, kvh)


def _context_attention(p, attn_lat, sink, aux, *, seq, ctx_len, mix):
    ctx_k, ctx_v, smem, out_shape, kvh = aux
    b = p.shape[0]
    blk = ATTN_BLOCK
    base = seq // blk
    return pl.pallas_call(
        functools.partial(_attn_ctx_kernel, windowed=False, nblk=0, kvh=kvh, ctx_len=ctx_len),
        grid=(b, ctx_len // blk),
        in_specs=[pl.BlockSpec((1, blk, mix), lambda i, n: (i, base + n, 5)),
                  ctx_k, ctx_v, smem,
                  pl.BlockSpec(memory_space=pl.ANY)],
        out_specs=pl.BlockSpec((1, blk, mix), lambda i, n: (i, base + n, 0)),
        out_shape=out_shape,
        input_output_aliases={4: 0},
        compiler_params=_params(("parallel", "arbitrary")),
        name="attn_context",
    )(p, p, p, sink, attn_lat)


def _attn_ctx_kernel(q_ref, kx_ref, vx_ref, sink_ref, prev_ref, o_ref, **kw):
    del prev_ref
    _attn_kernel(q_ref, kx_ref, vx_ref, sink_ref, o_ref, **kw)


def _attn_window_kernel(*refs, nblk, **kw):
    n = pl.program_id(1)
    o_ref = refs[-1]

    @pl.when(n < nblk)
    def _():
        _attn_kernel(*refs, nblk=nblk, **kw)

    @pl.when(n >= nblk)
    def _():
        o_ref[...] = jnp.zeros_like(o_ref)


def _merge_kernel(a_ref, t_ref, ga_ref, gb_ref, wa_ref, wb_ref, o_ref):
    ya = _dot(a_ref[0], wa_ref[...])
    yb = _dot(t_ref[0], wb_ref[...])
    y = (jax.nn.sigmoid(ga_ref[0].astype(F32)) * ya
         + jax.nn.sigmoid(gb_ref[0].astype(F32)) * yb)
    o_ref[0] = y.astype(o_ref.dtype)


def _merge(a, attn, p, wa, wb, *, tm, ntiles, tn, ga_col, gb_col):
    b, r, mix = a.shape
    d = wa.shape[1]
    return pl.pallas_call(
        _merge_kernel,
        grid=(b, ntiles, d // tn),
        in_specs=[pl.BlockSpec((1, tm, mix), lambda i, t, j: (i, t, 0)),
                  pl.BlockSpec((1, tm, mix), lambda i, t, j: (i, t, 0)),
                  pl.BlockSpec((1, tm, tn), lambda i, t, j: (i, t, ga_col + j)),
                  pl.BlockSpec((1, tm, tn), lambda i, t, j: (i, t, gb_col + j)),
                  pl.BlockSpec((mix, tn), lambda i, t, j: (0, j)),
                  pl.BlockSpec((mix, tn), lambda i, t, j: (0, j))],
        out_specs=pl.BlockSpec((1, tm, tn), lambda i, t, j: (i, t, j)),
        out_shape=jax.ShapeDtypeStruct((b, ntiles * tm, d), BF16),
        compiler_params=_params(("parallel", "parallel", "arbitrary")),
        name="branch_merge",
    )(a, attn, p, p, wa, wb)


def _out_proj_kernel(y_ref, w_ref, x_ref, modl_ref, modc_ref, o_ref, *, tm, seq):
    is_ctx = _is_ctx_rows(pl.program_id(1) * tm, tm, seq)
    gate = jnp.where(is_ctx, modc_ref[0, 2:3, :], modl_ref[0, 2:3, :])
    o_ref[0] = x_ref[0] + gate * _dot(y_ref[0], w_ref[...])


def _out_proj(y, w, x, mods, *, seq, tm, ntiles, tn):
    b, r, d = x.shape
    nb = mods.shape[0] - 1
    return pl.pallas_call(
        functools.partial(_out_proj_kernel, tm=tm, seq=seq),
        grid=(b, ntiles, d // tn),
        in_specs=[pl.BlockSpec((1, tm, d), lambda i, t, j: (i, t, 0)),
                  pl.BlockSpec((d, tn), lambda i, t, j: (0, j)),
                  pl.BlockSpec((1, tm, tn), lambda i, t, j: (i, t, j)),
                  pl.BlockSpec((1, 6, tn), lambda i, t, j: (i, 0, j)),
                  pl.BlockSpec((1, 6, tn), lambda i, t, j: (nb, 0, j))],
        out_specs=pl.BlockSpec((1, tm, tn), lambda i, t, j: (i, t, j)),
        out_shape=jax.ShapeDtypeStruct((b, ntiles * tm, d), F32),
        compiler_params=_params(("parallel", "parallel", "arbitrary")),
        name="out_proj",
    )(y, w, x, mods, mods)


def _ffn_kernel(x_ref, modl_ref, modc_ref, g_ref, wg_ref, wu_ref, wd_ref, o_ref, h_ref, acc_ref,
                *, tm, seq):
    j = pl.program_id(2)
    last = pl.num_programs(2) - 1
    tile_row0 = pl.program_id(1) * tm

    @pl.when(j == 0)
    def _():
        _norm_mod_rows(x_ref, h_ref, g_ref, modl_ref, modc_ref, 3, tile_row0, tm, seq)
        acc_ref[...] = jnp.zeros_like(acc_ref)

    h = h_ref[...]
    act = jax.nn.silu(_dot(h, wg_ref[...])) * _dot(h, wu_ref[...])
    acc_ref[...] += _dot(act.astype(BF16), wd_ref[...])

    @pl.when(j == last)
    def _():
        def body(r0, rc):
            rows = pl.ds(r0, rc)
            is_ctx = _is_ctx_rows(tile_row0 + r0, rc, seq)
            o_ref[0, rows, :] = (x_ref[0, rows, :]
                                 + _mod_row(modl_ref, modc_ref, 5, is_ctx) * acc_ref[rows, :])

        _for_row_chunks(tm, body)


def _ffn(x, mods, gain, wg, wu, wd, *, seq, tm, tf):
    b, r, d = x.shape
    f = wg.shape[1]
    nb = mods.shape[0] - 1
    return pl.pallas_call(
        functools.partial(_ffn_kernel, tm=tm, seq=seq),
        grid=(b, r // tm, f // tf),
        in_specs=[pl.BlockSpec((1, tm, d), lambda i, t, j: (i, t, 0)),
                  pl.BlockSpec((1, 6, d), lambda i, t, j: (i, 0, 0)),
                  pl.BlockSpec((1, 6, d), lambda i, t, j: (nb, 0, 0)),
                  pl.BlockSpec((1, d), lambda i, t, j: (0, 0)),
                  pl.BlockSpec((d, tf), lambda i, t, j: (0, j)),
                  pl.BlockSpec((d, tf), lambda i, t, j: (0, j)),
                  pl.BlockSpec((tf, d), lambda i, t, j: (j, 0))],
        out_specs=pl.BlockSpec((1, tm, d), lambda i, t, j: (i, t, 0)),
        out_shape=jax.ShapeDtypeStruct((b, r, d), F32),
        scratch_shapes=[pltpu.VMEM((tm, d), BF16), pltpu.VMEM((tm, d), F32)],
        compiler_params=_params(("parallel", "parallel", "arbitrary")),
        name="ffn_dense",
    )(x, mods, mods, gain, wg, wu, wd)


def _route_kernel(x_ref, modl_ref, g_ref, wr_ref, h_ref, r_ref, *, n_experts, tm):
    def body(r0, rc):
        rows = pl.ds(r0, rc)
        h = _norm_mod(x_ref[0, rows, :], g_ref[...], modl_ref[0, 3:4, :], modl_ref[0, 4:5, :])
        h_ref[rows, :] = h

    _for_row_chunks(tm, body)

    logits = jnp.dot(h_ref[...], wr_ref[...], preferred_element_type=F32,
                     precision=lax.Precision.HIGHEST)
    lane = lax.broadcasted_iota(jnp.int32, logits.shape, 1)
    lg = jnp.where(lane < n_experts, logits, -jnp.inf)
    m1 = jnp.max(lg, axis=-1, keepdims=True)
    i1 = jnp.min(jnp.where(lg == m1, lane, LANES), axis=-1, keepdims=True)
    lg2 = jnp.where(lane == i1, -jnp.inf, lg)
    m2 = jnp.max(lg2, axis=-1, keepdims=True)
    i2 = jnp.min(jnp.where(lg2 == m2, lane, LANES), axis=-1, keepdims=True)
    e2 = jnp.exp(m2 - m1)
    w1 = 1.0 / (1.0 + e2)
    w2 = e2 / (1.0 + e2)
    r_ref[...] = jnp.where(lane == 0, i1.astype(F32),
                           jnp.where(lane == 1, i2.astype(F32),
                                     jnp.where(lane == 2, w1, jnp.where(lane == 3, w2, 0.0))))


def _route(x, mods, gain, wr, *, seq, tm, n_experts):
    b, r, d = x.shape
    nt = seq // tm
    return pl.pallas_call(
        functools.partial(_route_kernel, n_experts=n_experts, tm=tm),
        grid=(b, nt),
        in_specs=[pl.BlockSpec((1, tm, d), lambda i, t: (i, t, 0)),
                  pl.BlockSpec((1, 6, d), lambda i, t: (i, 0, 0)),
                  pl.BlockSpec((1, d), lambda i, t: (0, 0)),
                  pl.BlockSpec((d, LANES), lambda i, t: (0, 0))],
        out_specs=[pl.BlockSpec((tm, d), lambda i, t: (i * nt + t, 0)),
                   pl.BlockSpec((tm, LANES), lambda i, t: (i * nt + t, 0))],
        out_shape=[jax.ShapeDtypeStruct((b * seq, d), F32),
                   jax.ShapeDtypeStruct((b * seq, LANES), F32)],
        compiler_params=_params(("parallel", "parallel")),
        name="moe_route",
    )(x, mods, gain, wr)


def _moe_ffn_kernel(te_ref, nu_ref, idx0_ref, idxn_ref, h_hbm, wg_ref, wu_ref, wd_ref, o_ref,
                    xbuf, hb_ref, sem, *, tm, nsteps):
    del te_ref
    i, j = pl.program_id(0), pl.program_id(1)
    nu = nu_ref[0]
    used = i < nu
    slot = i % 2
    per = tm // nsteps // 8 * 8
    rem = tm - per * nsteps

    def row_copy(idx_ref, s, r):
        return pltpu.make_async_copy(h_hbm.at[pl.ds(idx_ref[0, 0, r], 1), :],
                                     xbuf.at[s, pl.ds(r, 1), :], sem.at[s])

    def issue(idx_ref, s, lo, n):
        def body(k, carry):
            row_copy(idx_ref, s, lo + k).start()
            return carry

        lax.fori_loop(0, n, body, 0, unroll=8)

    @pl.when((i == 0) & (j == 0))
    def _():
        issue(idx0_ref, 0, 0, tm)

    @pl.when(used & (j == 0))
    def _():
        pltpu.make_async_copy(h_hbm.at[pl.ds(0, tm), :], xbuf.at[slot], sem.at[slot]).wait()

        def cast(r0, rc):
            hb_ref[pl.ds(r0, rc), :] = xbuf[slot, pl.ds(r0, rc), :].astype(BF16)

        _for_row_chunks(tm, cast)

    @pl.when(i + 1 < nu)
    def _():
        issue(idxn_ref, 1 - slot, j * per, per)

    if rem:
        @pl.when((i + 1 < nu) & (j == 0))
        def _():
            issue(idxn_ref, 1 - slot, nsteps * per, rem)

    @pl.when(j == 0)
    def _():
        o_ref[...] = jnp.zeros_like(o_ref)

    @pl.when(used)
    def _():
        h = hb_ref[...]
        act = jax.nn.silu(_dot(h, wg_ref[0])) * _dot(h, wu_ref[0])
        o_ref[...] += _dot(act.astype(BF16), wd_ref[0])


def _moe_ffn(h, src, tile_expert, n_used, wg, wu, wd, *, tm, tf):
    ns = src.shape[0]
    d = h.shape[1]
    f = wg.shape[2]
    nt = ns // tm

    def w_up(i, j, te, nu):
        return (te[i], 0, jnp.where(i < nu[0], j, 0))

    def w_down(i, j, te, nu):
        return (te[i], jnp.where(i < nu[0], j, 0), 0)

    smem = pltpu.SMEM
    grid_spec = pltpu.PrefetchScalarGridSpec(
        num_scalar_prefetch=2,
        grid=(nt, f // tf),
        in_specs=[pl.BlockSpec((1, 1, tm), lambda i, j, te, nu: (0, 0, 0), memory_space=smem),
                  pl.BlockSpec((1, 1, tm), lambda i, j, te, nu: (jnp.minimum(i + 1, nt - 1), 0, 0),
                               memory_space=smem),
                  pl.BlockSpec(memory_space=pl.ANY),
                  pl.BlockSpec((1, d, tf), w_up),
                  pl.BlockSpec((1, d, tf), w_up),
                  pl.BlockSpec((1, tf, d), w_down)],
        out_specs=pl.BlockSpec((tm, d), lambda i, j, te, nu: (i, 0)),
        scratch_shapes=[pltpu.VMEM((2, tm, d), F32), pltpu.VMEM((tm, d), BF16),
                        pltpu.SemaphoreType.DMA((2,))])
    src3 = src.reshape(nt, 1, tm)
    return pl.pallas_call(
        functools.partial(_moe_ffn_kernel, tm=tm, nsteps=f // tf),
        grid_spec=grid_spec,
        out_shape=jax.ShapeDtypeStruct((ns, d), F32),
        compiler_params=_params(("arbitrary", "arbitrary")),
        name="moe_ffn",
    )(tile_expert, n_used, src3, src3, h, wg, wu, wd)


def _combine_kernel(s0f_ref, s1f_ref, s0n_ref, s1n_ref, x_ref, modl_ref, r_ref, fn_ref, y_ref, o_ref,
                    ya_ref, yb_ref, sem, *, cm):
    s = pl.program_id(0)
    nsteps = pl.num_programs(0)
    slot = s % 2

    def issue(i0_ref, i1_ref, sl):
        def body(r, carry):
            pltpu.make_async_copy(y_ref.at[pl.ds(i0_ref[0, 0, r], 1), :],
                                  ya_ref.at[sl, pl.ds(r, 1), :], sem.at[sl, 0]).start()
            pltpu.make_async_copy(y_ref.at[pl.ds(i1_ref[0, 0, r], 1), :],
                                  yb_ref.at[sl, pl.ds(r, 1), :], sem.at[sl, 1]).start()
            return carry

        lax.fori_loop(0, cm, body, 0, unroll=8)

    @pl.when(s == 0)
    def _():
        issue(s0f_ref, s1f_ref, 0)

    @pl.when(s + 1 < nsteps)
    def _():
        issue(s0n_ref, s1n_ref, 1 - slot)

    pltpu.make_async_copy(y_ref.at[pl.ds(0, cm), :], ya_ref.at[slot], sem.at[slot, 0]).wait()
    pltpu.make_async_copy(y_ref.at[pl.ds(0, cm), :], yb_ref.at[slot], sem.at[slot, 1]).wait()

    def body(r0, rc):
        rows = pl.ds(r0, rc)
        route = r_ref[rows, :]
        moe = route[:, 2:3] * ya_ref[slot, rows, :] + route[:, 3:4] * yb_ref[slot, rows, :]
        xn = x_ref[0, rows, :] + modl_ref[0, 5:6, :] * moe
        ms = jnp.mean(xn * xn, axis=-1, keepdims=True)
        o_ref[0, rows, :] = xn * lax.rsqrt(ms + EPS) * fn_ref[...]

    _for_row_chunks(cm, body)


def _combine(x, mods, route, final_gain, y, slot0, slot1, *, seq, cm):
    b, r, d = x.shape
    nt = seq // cm
    nsteps = b * nt
    smem = pltpu.SMEM

    def first():
        return pl.BlockSpec((1, 1, cm), lambda s: (0, 0, 0), memory_space=smem)

    def ahead():
        return pl.BlockSpec((1, 1, cm), lambda s: (jnp.minimum(s + 1, nsteps - 1), 0, 0),
                            memory_space=smem)

    s0 = slot0.reshape(nsteps, 1, cm)
    s1 = slot1.reshape(nsteps, 1, cm)
    return pl.pallas_call(
        functools.partial(_combine_kernel, cm=cm),
        grid=(nsteps,),
        in_specs=[first(), first(), ahead(), ahead(),
                  pl.BlockSpec((1, cm, d), lambda s: (s // nt, s % nt, 0)),
                  pl.BlockSpec((1, 6, d), lambda s: (s // nt, 0, 0)),
                  pl.BlockSpec((cm, LANES), lambda s: (s, 0)),
                  pl.BlockSpec((1, d), lambda s: (0, 0)),
                  pl.BlockSpec(memory_space=pl.ANY)],
        out_specs=pl.BlockSpec((1, cm, d), lambda s: (s // nt, s % nt, 0)),
        out_shape=jax.ShapeDtypeStruct((b, seq, d), F32),
        scratch_shapes=[pltpu.VMEM((2, cm, d), F32), pltpu.VMEM((2, cm, d), F32),
                        pltpu.SemaphoreType.DMA((2, 2))],
        compiler_params=_params(("arbitrary",)),
        name="moe_combine",
    )(s0, s1, s0, s1, x, mods, route, final_gain, y)


def _moe_plan(route, n_experts, tm):
    t = route.shape[0]
    eid = route[:, :TOP_K].astype(jnp.int32).reshape(-1)
    onehot = (eid[:, None] == jnp.arange(n_experts, dtype=jnp.int32)[None, :]).astype(jnp.int32)
    csum = jnp.cumsum(onehot, axis=0)
    rank = jnp.sum((csum - onehot) * onehot, axis=1)
    counts = csum[-1]
    padded = ((counts + tm - 1) // tm) * tm
    ends = jnp.cumsum(padded)
    starts = ends - padded
    slot = starts[eid] + rank
    ns = TOP_K * t + n_experts * tm
    token = jnp.arange(TOP_K * t, dtype=jnp.int32) // TOP_K
    src = jnp.zeros((ns,), jnp.int32).at[slot].set(token)
    tile_start = jnp.arange(ns // tm, dtype=jnp.int32) * tm
    tile_expert = jnp.minimum(jnp.sum((tile_start[:, None] >= ends[None, :]).astype(jnp.int32), axis=1),
                              n_experts - 1)
    n_used = (ends[-1] // tm).astype(jnp.int32).reshape(1)
    slots = slot.reshape(t, TOP_K)
    return src, tile_expert, n_used, slots[:, 0], slots[:, 1]


def _rope_tables(length):
    rows = length // GRID_W
    row = jnp.repeat(jnp.arange(rows, dtype=F32), GRID_W)
    col = jnp.tile(jnp.arange(GRID_W, dtype=F32), rows)
    n_freq = ATTN_HEAD_DIM // 4
    inv = ROPE_THETA ** (-jnp.arange(n_freq, dtype=F32) / n_freq)
    ar, ac = row[:, None] * inv, col[:, None] * inv
    cos = jnp.concatenate([jnp.cos(ar), jnp.cos(ar), jnp.cos(ac), jnp.cos(ac)], axis=1)
    sin = jnp.concatenate([-jnp.sin(ar), jnp.sin(ar), -jnp.sin(ac), jnp.sin(ac)], axis=1)
    reps = LANES // ATTN_HEAD_DIM
    return jnp.tile(cos, (1, reps)), jnp.tile(sin, (1, reps))


def _lower_bound(raw, layer):
    p = jax.nn.softmax(raw.astype(F32), axis=0)
    return (jnp.cumsum(p, axis=0) - p[0])[layer]


def kernel(x, c, ctx, c_ctx, w_mod, b_mod, norm_mix, norm_ffn, w_in, hg_lb_fwd, hg_lb_bwd, hg_norm,
           attn_sink, w_branch_a, w_branch_b, w_out, ffn_w_gate, ffn_w_up, ffn_w_down,
           moe_router, moe_w_gate, moe_w_up, moe_w_down, final_norm):
    b, seq, d = x.shape
    ctx_len = ctx.shape[1]
    depth = w_mod.shape[0]
    r = seq + ctx_len
    mix = d // 2
    kvw = (mix // ATTN_HEAD_DIM // ATTN_GROUP) * ATTN_HEAD_DIM
    n_experts = moe_router.shape[-1]
    assert depth == 2 and kvw % LANES == 0 and seq % ctx_len == 0
    assert seq % HG_CHUNK == 0 and ctx_len % HG_CHUNK == 0 and seq % GRID_W == 0

    tn = 512
    ga = -(-(6 * mix + 2 * kvw) // tn) * tn
    gb = ga + d
    n_in = 6 * mix + 2 * kvw
    tn_in = _pick_tile(gb + d, 768, 2 * LANES)
    tm_full = _pick_tile(r, 1152)
    tm_lat = _pick_tile(seq, 1024)
    tm_ffn = _pick_tile(r, 768)

    mb = -(-(b + 1) // 8) * 8
    cvec = jnp.zeros((mb, d), F32).at[:b].set(c).at[b].set(c_ctx)
    mods_all = _modulation(cvec, w_mod, b_mod)[:, :b + 1].reshape(depth, b + 1, 6, d)

    cos, sin = _rope_tables(seq)
    xs = jnp.concatenate([x, ctx], axis=1)

    for layer in range(depth):
        need_ctx = layer < depth - 1
        mods = mods_all[layer]
        w = w_in[layer]
        wp = jnp.concatenate([w[:, :n_in], jnp.zeros((d, ga - n_in), w.dtype), w[:, n_in:]],
                             axis=1).astype(BF16)
        p = _in_proj(xs, mods, norm_mix[layer].reshape(1, d), wp, seq=seq, tm=tm_full, tn=tn_in)

        lb_f = _lower_bound(hg_lb_fwd, layer).reshape(1, mix)
        lb_b = _lower_bound(hg_lb_bwd, layer).reshape(1, mix)
        o_b = _hgrn(p, lb_b, seq=seq, ctx_len=ctx_len, mix=mix, rev=True, fcol=2)
        a = _hgrn(p, lb_f, seq=seq, ctx_len=ctx_len, mix=mix, rev=False, fcol=1,
                  extra=(o_b, hg_norm[layer].reshape(1, mix)))

        attn, aux = _attention(p, cos, sin, attn_sink[layer], seq=seq, ctx_len=ctx_len, mix=mix, kvw=kvw)
        if need_ctx:
            attn = _context_attention(p, attn, attn_sink[layer], aux, seq=seq, ctx_len=ctx_len, mix=mix)

        tm, nt = (tm_full, r // tm_full) if need_ctx else (tm_lat, seq // tm_lat)
        y = _merge(a, attn, p, w_branch_a[layer].astype(BF16), w_branch_b[layer].astype(BF16),
                   tm=tm, ntiles=nt, tn=tn, ga_col=ga // tn, gb_col=gb // tn)
        xs = _out_proj(y, w_out[layer].astype(BF16), xs, mods, seq=seq, tm=tm, ntiles=nt, tn=tn)

        i = layer // 2
        gain = norm_ffn[layer].reshape(1, d)
        if layer % 2 == 0:
            tf = _pick_tile(ffn_w_gate.shape[-1], 512, LANES)
            xs = _ffn(xs, mods, gain, ffn_w_gate[i].astype(BF16), ffn_w_up[i].astype(BF16),
                      ffn_w_down[i].astype(BF16), seq=seq, tm=tm_ffn, tf=tf)
        else:
            tme = _pick_tile(seq, 512)
            wr = jnp.zeros((d, LANES), F32).at[:, :n_experts].set(moe_router[i])
            h, route = _route(xs, mods, gain, wr, seq=seq, tm=tme, n_experts=n_experts)
            src, tile_expert, n_used, slot0, slot1 = _moe_plan(route, n_experts, tme)
            tf = _pick_tile(moe_w_gate.shape[-1], 1024, LANES)
            yg = _moe_ffn(h, src, tile_expert, n_used, moe_w_gate[i].astype(BF16),
                          moe_w_up[i].astype(BF16), moe_w_down[i].astype(BF16), tm=tme, tf=tf)
            out = _combine(xs, mods, route, final_norm.reshape(1, d), yg, slot0, slot1,
                           seq=seq, cm=_pick_tile(seq, 256))
    return out
```

```python
import functools

import jax
import jax.numpy as jnp
import numpy as np
from jax import lax
from jax.experimental import pallas as pl
from jax.experimental.pallas import tpu as pltpu

HG_HEAD_DIM = 128
ATTN_HEAD_DIM = 64
ATTN_GROUP = 8
ATTN_BLOCK = 128
GRID_W = 64
ROPE_THETA = 10000.0
EPS = 1e-6
TOP_K = 2
LANES = 128
HG_CHUNK = 128
HG_FINE = 8
VMEM_LIMIT = 56 * 1024 * 1024

F32 = jnp.float32
BF16 = jnp.bfloat16


def _pick_tile(n, cap, mult=16):
    best = None
    for t in range(mult, min(cap, n) + 1, mult):
        if n % t == 0:
            best = t
    assert best is not None, (n, cap)
    return best


def _batch_rows_per_step(b):
    return next(k for k in (4, 2, 1) if b % k == 0)


def _params(sem, vmem=VMEM_LIMIT, **kw):
    return pltpu.CompilerParams(dimension_semantics=sem, vmem_limit_bytes=vmem, **kw)


def _dot(a, b):
    return jnp.dot(a, b, preferred_element_type=F32)


def _dot_nt(a, b):
    return lax.dot_general(a, b, (((1,), (1,)), ((), ())), preferred_element_type=F32)


def _dot_tn(a, b):
    return lax.dot_general(a, b, (((0,), (0,)), ((), ())), preferred_element_type=F32)


def _norm_mod(x, gain, shift, scale):
    ms = jnp.mean(x * x, axis=-1, keepdims=True)
    return x * lax.rsqrt(ms + EPS) * gain * (1.0 + scale) + shift


def _is_ctx_rows(row0, n, seq):
    rows = row0 + lax.broadcasted_iota(jnp.int32, (n, 1), 0)
    return rows >= seq


def _mod_row(modl_ref, modc_ref, k, is_ctx):
    return jnp.where(is_ctx, modc_ref[0, k:k + 1, :], modl_ref[0, k:k + 1, :])


def _for_row_chunks(tm, body):
    rc = _pick_tile(tm, 32)

    def step(k, carry):
        body(pl.multiple_of(k * rc, rc), rc)
        return carry

    lax.fori_loop(0, tm // rc, step, 0)


def _norm_mod_rows(x_ref, h_ref, g_ref, modl_ref, modc_ref, k_shift, tile_row0, tm, seq):
    def body(r0, rc):
        rows = pl.ds(r0, rc)
        is_ctx = _is_ctx_rows(tile_row0 + r0, rc, seq)
        h = _norm_mod(x_ref[0, rows, :], g_ref[...], _mod_row(modl_ref, modc_ref, k_shift, is_ctx),
                      _mod_row(modl_ref, modc_ref, k_shift + 1, is_ctx))
        h_ref[rows, :] = h.astype(h_ref.dtype)

    _for_row_chunks(tm, body)


def _mod_kernel(c_ref, w_ref, b_ref, o_ref):
    act = jax.nn.silu(c_ref[...])
    o_ref[0] = jnp.dot(act, w_ref[0], preferred_element_type=F32,
                       precision=lax.Precision.HIGHEST) + b_ref[0]


def _modulation(cvec, w_mod, b_mod):
    depth, d, n6 = w_mod.shape
    mb = cvec.shape[0]
    tn = _pick_tile(n6, 1024, LANES)
    return pl.pallas_call(
        _mod_kernel,
        grid=(depth, n6 // tn),
        in_specs=[pl.BlockSpec((mb, d), lambda l, j: (0, 0)),
                  pl.BlockSpec((1, d, tn), lambda l, j: (l, 0, j)),
                  pl.BlockSpec((1, 1, tn), lambda l, j: (l, 0, j))],
        out_specs=pl.BlockSpec((1, mb, tn), lambda l, j: (l, 0, j)),
        out_shape=jax.ShapeDtypeStruct((depth, mb, n6), F32),
        compiler_params=_params(("parallel", "arbitrary")),
        name="modulation",
    )(cvec, w_mod, b_mod.reshape(depth, 1, n6))


def _in_proj_kernel(x_ref, modl_ref, modc_ref, g_ref, w_ref, o_ref, h_ref, *, tm, seq):
    tile_row0 = pl.program_id(1) * tm

    @pl.when(pl.program_id(2) == 0)
    def _():
        _norm_mod_rows(x_ref, h_ref, g_ref, modl_ref, modc_ref, 0, tile_row0, tm, seq)

    o_ref[0] = _dot(h_ref[...], w_ref[...]).astype(o_ref.dtype)


def _in_proj(x, mods, gain, w, *, seq, tm, tn):
    b, r, d = x.shape
    npad = w.shape[1]
    nb = mods.shape[0] - 1
    return pl.pallas_call(
        functools.partial(_in_proj_kernel, tm=tm, seq=seq),
        grid=(b, r // tm, npad // tn),
        in_specs=[pl.BlockSpec((1, tm, d), lambda i, t, j: (i, t, 0)),
                  pl.BlockSpec((1, 6, d), lambda i, t, j: (i, 0, 0)),
                  pl.BlockSpec((1, 6, d), lambda i, t, j: (nb, 0, 0)),
                  pl.BlockSpec((1, d), lambda i, t, j: (0, 0)),
                  pl.BlockSpec((d, tn), lambda i, t, j: (0, j))],
        out_specs=pl.BlockSpec((1, tm, tn), lambda i, t, j: (i, t, j)),
        out_shape=jax.ShapeDtypeStruct((b, r, npad), BF16),
        scratch_shapes=[pltpu.VMEM((tm, d), BF16)],
        compiler_params=_params(("parallel", "parallel", "arbitrary")),
        name="in_proj",
    )(x, mods, mods, gain, w)


def _hgrn_tables(c, rev):
    nlev = c.bit_length() - 1
    t = np.arange(c)[:, None]
    u = np.arange(c)[None, :]
    fams = [u <= t]
    level = np.full((c, c), -1)
    for li in range(nlev):
        half = c >> (li + 1)
        blk = 2 * half
        start = (t // blk) * blk
        qside = (t % blk) >= half
        if half <= HG_FINE:
            fams.append(np.where(qside, (u >= start + half) & (u <= t), (u > t) & (u < start + half)))
        level = np.where((t // blk == u // blk) & qside & ((u % blk) < half), li, level)
    sums = np.concatenate(fams, axis=0).astype(np.float32)
    if rev:
        sums = sums.reshape(len(fams), c, c)[:, ::-1, ::-1].reshape(len(fams) * c, c)
        level = level[::-1, ::-1]
    return (jnp.asarray(np.tile(sums, (1, 2)), BF16), jnp.asarray(np.ascontiguousarray(level), jnp.int32))


def _hgrn_kernel(*refs, rev, chunk, nheads, final, nb):
    if final:
        q_ref, f_ref, v_ref, lb_ref, sums_ref, lvl_ref, ob_ref, g_ref, gain_ref, o_ref, st_ref = refs
    else:
        q_ref, f_ref, v_ref, lb_ref, sums_ref, lvl_ref, o_ref, st_ref = refs
    c = chunk
    hd = HG_HEAD_DIM

    @pl.when(pl.program_id(1) == 0)
    def _():
        st_ref[...] = jnp.zeros_like(st_ref)

    lb = lb_ref[...]
    rows = lax.broadcasted_iota(jnp.int32, (c, 1), 0)
    level = lvl_ref[...]
    scale = HG_HEAD_DIM ** -0.5
    last = 0 if rev else c - 1
    nlev = c.bit_length() - 1
    in_level = [level == li for li in range(nlev)]
    for bi in range(nb):
        _hgrn_chunk(bi, q_ref, f_ref, v_ref, lb, sums_ref, o_ref, st_ref,
                    (ob_ref, g_ref, gain_ref) if final else None,
                    rows, in_level, scale, last, nlev, c, hd, nheads, rev)


def _hgrn_chunk(bi, q_ref, f_ref, v_ref, lb, sums_ref, o_ref, st_ref, merge_refs,
                rows, in_level, scale, last, nlev, c, hd, nheads, rev):
    final = merge_refs is not None
    if final:
        ob_ref, g_ref, gain_ref = merge_refs
    f = lb + (1.0 - lb) * jax.nn.sigmoid(f_ref[bi].astype(F32))
    g2 = jnp.log2(f)
    g_hi = g2.astype(BF16)
    g_lo = (g2 - g_hi.astype(F32)).astype(BF16)
    expo = _dot(sums_ref[...], jnp.concatenate([g_hi, g_lo], axis=0))
    outs, states = [], []

    for h in range(nheads):
        sl = slice(h * hd, (h + 1) * hd)
        qh = q_ref[bi, :, sl].astype(F32) * scale
        kh = 1.0 - f[:, sl]
        vh = v_ref[bi, :, sl]
        bh = expo[0:c, sl]
        btot = bh[last:last + 1, :]
        eb = jnp.exp2(bh)
        st = st_ref[bi, h]

        scores = jnp.zeros((c, c), F32)
        fine = 0
        for li in range(nlev):
            half = c >> (li + 1)
            blk = 2 * half
            qside = ((rows & half) == 0) if rev else ((rows & half) != 0)
            if half > HG_FINE:
                ref0 = half if rev else half - 1
                bref = bh[ref0:ref0 + 1, :]
                for k in range(1, c // blk):
                    bref = jnp.where(rows >= k * blk, bh[k * blk + ref0:k * blk + ref0 + 1, :], bref)
                x = bh - bref
                x = jnp.where(qside, x, -x)
            else:
                fine += 1
                x = expo[fine * c:(fine + 1) * c, sl]
            z = (jnp.where(qside, qh, kh) * jnp.exp2(x)).astype(BF16)
            scores = jnp.where(in_level[li], _dot_nt(z, z), scores)

        lhs = jnp.concatenate([(qh * eb).astype(BF16), scores.astype(BF16)], axis=1)
        rhs = jnp.concatenate([st.T.astype(BF16), vh], axis=0)
        o = _dot(lhs, rhs) + jnp.sum(qh * kh, axis=-1, keepdims=True) * vh.astype(F32)

        khat = (kh * jnp.exp2(btot - bh)).astype(BF16)
        states.append(st * eb[last:last + 1, :] + _dot_tn(vh, khat))

        if final:
            osum = o + ob_ref[bi, :, sl]
            ms = jnp.mean(osum * osum, axis=-1, keepdims=True)
            o = (osum * lax.rsqrt(ms + EPS) * gain_ref[:, sl]
                 * jax.nn.silu(g_ref[bi, :, sl].astype(F32)))
        outs.append(o.astype(o_ref.dtype))

    o_ref[bi] = jnp.concatenate(outs, axis=1)
    st_ref[bi] = jnp.stack(states, axis=0)


def _hgrn(p, lb, *, seq, ctx_len, mix, rev, fcol, extra=None):
    b, r, _ = p.shape
    c = HG_CHUNK
    nl, nc = seq // c, ctx_len // c
    nheads = mix // HG_HEAD_DIM
    nb = _batch_rows_per_step(b)

    if rev:
        def order(n):
            return nl + nc - 1 - n
    else:
        def order(n):
            return jnp.where(n < nc, nl + n, n - nc)

    def pspec(col):
        return pl.BlockSpec((nb, c, mix), lambda i, n: (i, order(n), col))

    sums, level = _hgrn_tables(c, rev)
    in_specs = [pspec(0), pspec(fcol), pspec(3), pl.BlockSpec((1, mix), lambda i, n: (0, 0)),
                pl.BlockSpec(sums.shape, lambda i, n: (0, 0)),
                pl.BlockSpec(level.shape, lambda i, n: (0, 0))]
    args = [p, p, p, lb, sums, level]
    final = extra is not None
    if final:
        o_other, gain = extra
        in_specs += [pl.BlockSpec((nb, c, mix), lambda i, n: (i, order(n), 0)), pspec(4),
                     pl.BlockSpec((1, mix), lambda i, n: (0, 0))]
        args += [o_other, p, gain]
    return pl.pallas_call(
        functools.partial(_hgrn_kernel, rev=rev, chunk=c, nheads=nheads, final=final, nb=nb),
        grid=(b // nb, nl + nc),
        in_specs=in_specs,
        out_specs=pl.BlockSpec((nb, c, mix), lambda i, n: (i, order(n), 0)),
        out_shape=jax.ShapeDtypeStruct((b, r, mix), BF16 if final else F32),
        scratch_shapes=[pltpu.VMEM((nb, nheads, HG_HEAD_DIM, HG_HEAD_DIM), F32)],
        compiler_params=_params(("parallel", "arbitrary")),
        name="hgrn_bwd" if rev else "hgrn_fwd",
    )(*args)


def _rope(x, cos, sin):
    lane = lax.broadcasted_iota(jnp.int32, x.shape, 1)
    first = (lane % 32) < 16
    swapped = jnp.where(first, pltpu.roll(x, LANES - 16, 1), pltpu.roll(x, 16, 1))
    return x * cos + swapped * sin


def _attn_kernel(*refs, windowed, nblk, kvh, ctx_len, bi=0):
    if windowed:
        (q_ref, kp_ref, kc_ref, kn_ref, vp_ref, vc_ref, vn_ref, kx_ref, vx_ref,
         cq_ref, sq_ref, cp_ref, sp_ref, cn_ref, sn_ref, sink_ref, o_ref) = refs
    else:
        q_ref, kx_ref, vx_ref, sink_ref, o_ref = refs
    blk = ATTN_BLOCK
    hd = ATTN_HEAD_DIM
    gw = ATTN_GROUP * hd
    n = pl.program_id(1)
    log2e = 1.4426950408889634
    scale = ATTN_HEAD_DIM ** -0.5 * log2e

    if windowed:
        kwin = [_rope(kp_ref[bi].astype(F32), cp_ref[...], sp_ref[...]),
                _rope(kc_ref[bi].astype(F32), cq_ref[...], sq_ref[...]),
                _rope(kn_ref[bi].astype(F32), cn_ref[...], sn_ref[...])]
        vwin = [vp_ref[bi], vc_ref[bi], vn_ref[bi]]
        nkeys = ctx_len + 3 * blk
        qi = lax.broadcasted_iota(jnp.int32, (blk, nkeys), 0)
        kj = lax.broadcasted_iota(jnp.int32, (blk, nkeys), 1) - ctx_len
        valid = (kj < 0) | ((kj >= qi) & (kj <= qi + 2 * blk)
                            & ((kj >= blk) | (n > 0)) & ((kj < 2 * blk) | (n < nblk - 1)))

    for g in range(kvh):
        ks = slice(g * hd, (g + 1) * hd)
        k_parts = [kx_ref[bi, :, ks]]
        v_parts = [vx_ref[bi, :, ks]]
        if windowed:
            k_parts += [kw[:, ks] for kw in kwin]
            v_parts += [vw[:, ks] for vw in vwin]
        k_all = jnp.concatenate(k_parts, axis=0).astype(BF16)
        v_all = jnp.concatenate(v_parts, axis=0).astype(BF16)

        q_heads = []
        for c4 in range(gw // LANES):
            col = g * gw + c4 * LANES
            qc = q_ref[bi, :, col:col + LANES].astype(F32)
            if windowed:
                qc = _rope(qc, cq_ref[...], sq_ref[...])
            qc = qc * scale
            q_heads += [qc[:, :hd], qc[:, hd:]]
        q_stack = jnp.concatenate(q_heads, axis=0).astype(BF16)
        s_all = _dot_nt(q_stack, k_all)

        p_list, inv_list = [], []
        for r in range(ATTN_GROUP):
            s = s_all[r * blk:(r + 1) * blk, :]
            if windowed:
                s = jnp.where(valid, s, -jnp.inf)
            sink = sink_ref[g * ATTN_GROUP + r] * log2e
            m = jnp.maximum(jnp.max(s, axis=-1, keepdims=True), sink)
            p = jnp.exp2(s - m)
            denom = jnp.sum(p, axis=-1, keepdims=True) + jnp.exp2(sink - m)
            p_list.append(p.astype(BF16))
            inv_list.append(1.0 / denom)
        o_all = _dot(jnp.concatenate(p_list, axis=0), v_all)
        outs = [o_all[r * blk:(r + 1) * blk, :] * inv_list[r] for r in range(ATTN_GROUP)]
        o_ref[bi, :, g * gw:(g + 1) * gw] = jnp.concatenate(outs, axis=1).astype(o_ref.dtype)


def _attention(p, cos, sin, sink, *, seq, ctx_len, mix, kvw):
    b, r, _ = p.shape
    blk = ATTN_BLOCK
    nblk = seq // blk
    kvh = kvw // ATTN_HEAD_DIM
    kcol = 6 * mix // kvw
    ctx_blk = seq // ctx_len
    nb = _batch_rows_per_step(b)

    def kv(col, off):
        return pl.BlockSpec((nb, blk, kvw),
                            lambda i, n: (i, jnp.clip(n + off, 0, nblk - 1), col))

    def tab(off):
        return pl.BlockSpec((blk, LANES), lambda i, n: (jnp.clip(n + off, 0, nblk - 1), 0))

    smem = pl.BlockSpec(memory_space=pltpu.SMEM)
    ctx_k = pl.BlockSpec((1, ctx_len, kvw), lambda i, n: (i, ctx_blk, kcol))
    ctx_v = pl.BlockSpec((1, ctx_len, kvw), lambda i, n: (i, ctx_blk, kcol + 1))
    out_shape = jax.ShapeDtypeStruct((b, r, mix), BF16)

    ctx_kb = pl.BlockSpec((nb, ctx_len, kvw), lambda i, n: (i, ctx_blk, kcol))
    ctx_vb = pl.BlockSpec((nb, ctx_len, kvw), lambda i, n: (i, ctx_blk, kcol + 1))
    lat = pl.pallas_call(
        functools.partial(_attn_window_kernel, windowed=True, nblk=nblk, kvh=kvh, ctx_len=ctx_len),
        grid=(b // nb, r // blk),
        in_specs=[pl.BlockSpec((nb, blk, mix), lambda i, n: (i, jnp.minimum(n, nblk - 1), 5)),
                  kv(kcol, -1), kv(kcol, 0), kv(kcol, 1),
                  kv(kcol + 1, -1), kv(kcol + 1, 0), kv(kcol + 1, 1),
                  ctx_kb, ctx_vb,
                  tab(0), tab(0), tab(-1), tab(-1), tab(1), tab(1), smem],
        out_specs=pl.BlockSpec((nb, blk, mix), lambda i, n: (i, n, 0)),
        out_shape=out_shape,
        compiler_params=_params(("parallel", "arbitrary")),
        name="attn_window",
    )(p, p, p, p, p, p, p, p, p, cos, sin, cos, sin, cos, sin, sink)
    return lat, (ctx_k, ctx_v, smem, out_shape, kvh)


def _context_attention(p, attn_lat, sink, aux, *, seq, ctx_len, mix):
    ctx_k, ctx_v, smem, out_shape, kvh = aux
    b = p.shape[0]
    blk = ATTN_BLOCK
    base = seq // blk
    return pl.pallas_call(
        functools.partial(_attn_ctx_kernel, windowed=False, nblk=0, kvh=kvh, ctx_len=ctx_len),
        grid=(b, ctx_len // blk),
        in_specs=[pl.BlockSpec((1, blk, mix), lambda i, n: (i, base + n, 5)),
                  ctx_k, ctx_v, smem,
                  pl.BlockSpec(memory_space=pl.ANY)],
        out_specs=pl.BlockSpec((1, blk, mix), lambda i, n: (i, base + n, 0)),
        out_shape=out_shape,
        input_output_aliases={4: 0},
        compiler_params=_params(("parallel", "arbitrary")),
        name="attn_context",
    )(p, p, p, sink, attn_lat)


def _attn_ctx_kernel(q_ref, kx_ref, vx_ref, sink_ref, prev_ref, o_ref, **kw):
    del prev_ref
    _attn_kernel(q_ref, kx_ref, vx_ref, sink_ref, o_ref, **kw)


def _attn_window_kernel(*refs, nblk, **kw):
    n = pl.program_id(1)
    o_ref = refs[-1]

    @pl.when(n < nblk)
    def _():
        for bi in range(o_ref.shape[0]):
            _attn_kernel(*refs, nblk=nblk, bi=bi, **kw)

    @pl.when(n >= nblk)
    def _():
        o_ref[...] = jnp.zeros_like(o_ref)


def _merge_kernel(a_ref, t_ref, ga_ref, gb_ref, wa_ref, wb_ref, o_ref):
    ya = _dot(a_ref[0], wa_ref[...])
    yb = _dot(t_ref[0], wb_ref[...])
    y = (jax.nn.sigmoid(ga_ref[0].astype(F32)) * ya
         + jax.nn.sigmoid(gb_ref[0].astype(F32)) * yb)
    o_ref[0] = y.astype(o_ref.dtype)


def _merge(a, attn, p, wa, wb, *, tm, ntiles, tn, ga_col, gb_col):
    b, r, mix = a.shape
    d = wa.shape[1]
    return pl.pallas_call(
        _merge_kernel,
        grid=(b, ntiles, d // tn),
        in_specs=[pl.BlockSpec((1, tm, mix), lambda i, t, j: (i, t, 0)),
                  pl.BlockSpec((1, tm, mix), lambda i, t, j: (i, t, 0)),
                  pl.BlockSpec((1, tm, tn), lambda i, t, j: (i, t, ga_col + j)),
                  pl.BlockSpec((1, tm, tn), lambda i, t, j: (i, t, gb_col + j)),
                  pl.BlockSpec((mix, tn), lambda i, t, j: (0, j)),
                  pl.BlockSpec((mix, tn), lambda i, t, j: (0, j))],
        out_specs=pl.BlockSpec((1, tm, tn), lambda i, t, j: (i, t, j)),
        out_shape=jax.ShapeDtypeStruct((b, ntiles * tm, d), BF16),
        compiler_params=_params(("parallel", "parallel", "arbitrary")),
        name="branch_merge",
    )(a, attn, p, p, wa, wb)


def _out_proj_kernel(y_ref, w_ref, x_ref, modl_ref, modc_ref, o_ref, *, tm, seq):
    is_ctx = _is_ctx_rows(pl.program_id(1) * tm, tm, seq)
    gate = jnp.where(is_ctx, modc_ref[0, 2:3, :], modl_ref[0, 2:3, :])
    o_ref[0] = x_ref[0] + gate * _dot(y_ref[0], w_ref[...])


def _out_proj(y, w, x, mods, *, seq, tm, ntiles, tn):
    b, r, d = x.shape
    nb = mods.shape[0] - 1
    return pl.pallas_call(
        functools.partial(_out_proj_kernel, tm=tm, seq=seq),
        grid=(b, ntiles, d // tn),
        in_specs=[pl.BlockSpec((1, tm, d), lambda i, t, j: (i, t, 0)),
                  pl.BlockSpec((d, tn), lambda i, t, j: (0, j)),
                  pl.BlockSpec((1, tm, tn), lambda i, t, j: (i, t, j)),
                  pl.BlockSpec((1, 6, tn), lambda i, t, j: (i, 0, j)),
                  pl.BlockSpec((1, 6, tn), lambda i, t, j: (nb, 0, j))],
        out_specs=pl.BlockSpec((1, tm, tn), lambda i, t, j: (i, t, j)),
        out_shape=jax.ShapeDtypeStruct((b, ntiles * tm, d), F32),
        compiler_params=_params(("parallel", "parallel", "arbitrary")),
        name="out_proj",
    )(y, w, x, mods, mods)


def _ffn_kernel(x_ref, modl_ref, modc_ref, g_ref, wg_ref, wu_ref, wd_ref, o_ref, h_ref, acc_ref,
                *, tm, seq):
    j = pl.program_id(2)
    last = pl.num_programs(2) - 1
    tile_row0 = pl.program_id(1) * tm

    @pl.when(j == 0)
    def _():
        _norm_mod_rows(x_ref, h_ref, g_ref, modl_ref, modc_ref, 3, tile_row0, tm, seq)
        acc_ref[...] = jnp.zeros_like(acc_ref)

    h = h_ref[...]
    act = jax.nn.silu(_dot(h, wg_ref[...])) * _dot(h, wu_ref[...])
    acc_ref[...] += _dot(act.astype(BF16), wd_ref[...])

    @pl.when(j == last)
    def _():
        def body(r0, rc):
            rows = pl.ds(r0, rc)
            is_ctx = _is_ctx_rows(tile_row0 + r0, rc, seq)
            o_ref[0, rows, :] = (x_ref[0, rows, :]
                                 + _mod_row(modl_ref, modc_ref, 5, is_ctx) * acc_ref[rows, :])

        _for_row_chunks(tm, body)


def _ffn(x, mods, gain, wg, wu, wd, *, seq, tm, tf):
    b, r, d = x.shape
    f = wg.shape[1]
    nb = mods.shape[0] - 1
    return pl.pallas_call(
        functools.partial(_ffn_kernel, tm=tm, seq=seq),
        grid=(b, r // tm, f // tf),
        in_specs=[pl.BlockSpec((1, tm, d), lambda i, t, j: (i, t, 0)),
                  pl.BlockSpec((1, 6, d), lambda i, t, j: (i, 0, 0)),
                  pl.BlockSpec((1, 6, d), lambda i, t, j: (nb, 0, 0)),
                  pl.BlockSpec((1, d), lambda i, t, j: (0, 0)),
                  pl.BlockSpec((d, tf), lambda i, t, j: (0, j)),
                  pl.BlockSpec((d, tf), lambda i, t, j: (0, j)),
                  pl.BlockSpec((tf, d), lambda i, t, j: (j, 0))],
        out_specs=pl.BlockSpec((1, tm, d), lambda i, t, j: (i, t, 0)),
        out_shape=jax.ShapeDtypeStruct((b, r, d), F32),
        scratch_shapes=[pltpu.VMEM((tm, d), BF16), pltpu.VMEM((tm, d), F32)],
        compiler_params=_params(("parallel", "parallel", "arbitrary")),
        name="ffn_dense",
    )(x, mods, mods, gain, wg, wu, wd)


def _route_kernel(x_ref, modl_ref, g_ref, wr_ref, h_ref, r_ref, *, n_experts, tm):
    def body(r0, rc):
        rows = pl.ds(r0, rc)
        h = _norm_mod(x_ref[0, rows, :], g_ref[...], modl_ref[0, 3:4, :], modl_ref[0, 4:5, :])
        h_ref[rows, :] = h

    _for_row_chunks(tm, body)

    logits = jnp.dot(h_ref[...], wr_ref[...], preferred_element_type=F32,
                     precision=lax.Precision.HIGHEST)
    lane = lax.broadcasted_iota(jnp.int32, logits.shape, 1)
    lg = jnp.where(lane < n_experts, logits, -jnp.inf)
    m1 = jnp.max(lg, axis=-1, keepdims=True)
    i1 = jnp.min(jnp.where(lg == m1, lane, LANES), axis=-1, keepdims=True)
    lg2 = jnp.where(lane == i1, -jnp.inf, lg)
    m2 = jnp.max(lg2, axis=-1, keepdims=True)
    i2 = jnp.min(jnp.where(lg2 == m2, lane, LANES), axis=-1, keepdims=True)
    e2 = jnp.exp(m2 - m1)
    w1 = 1.0 / (1.0 + e2)
    w2 = e2 / (1.0 + e2)
    r_ref[...] = jnp.where(lane == 0, i1.astype(F32),
                           jnp.where(lane == 1, i2.astype(F32),
                                     jnp.where(lane == 2, w1, jnp.where(lane == 3, w2, 0.0))))


def _route(x, mods, gain, wr, *, seq, tm, n_experts):
    b, r, d = x.shape
    nt = seq // tm
    return pl.pallas_call(
        functools.partial(_route_kernel, n_experts=n_experts, tm=tm),
        grid=(b, nt),
        in_specs=[pl.BlockSpec((1, tm, d), lambda i, t: (i, t, 0)),
                  pl.BlockSpec((1, 6, d), lambda i, t: (i, 0, 0)),
                  pl.BlockSpec((1, d), lambda i, t: (0, 0)),
                  pl.BlockSpec((d, LANES), lambda i, t: (0, 0))],
        out_specs=[pl.BlockSpec((tm, d), lambda i, t: (i * nt + t, 0)),
                   pl.BlockSpec((tm, LANES), lambda i, t: (i * nt + t, 0))],
        out_shape=[jax.ShapeDtypeStruct((b * seq, d), F32),
                   jax.ShapeDtypeStruct((b * seq, LANES), F32)],
        compiler_params=_params(("parallel", "parallel")),
        name="moe_route",
    )(x, mods, gain, wr)


def _moe_ffn_kernel(te_ref, nu_ref, idx0_ref, idxn_ref, h_hbm, wg_ref, wu_ref, wd_ref, o_ref,
                    xbuf, hb_ref, sem, *, tm, nsteps):
    del te_ref
    i, j = pl.program_id(0), pl.program_id(1)
    nu = nu_ref[0]
    used = i < nu
    slot = i % 2
    per = tm // nsteps // 8 * 8
    rem = tm - per * nsteps

    def row_copy(idx_ref, s, r):
        return pltpu.make_async_copy(h_hbm.at[pl.ds(idx_ref[0, 0, r], 1), :],
                                     xbuf.at[s, pl.ds(r, 1), :], sem.at[s])

    def issue(idx_ref, s, lo, n):
        def body(k, carry):
            row_copy(idx_ref, s, lo + k).start()
            return carry

        lax.fori_loop(0, n, body, 0, unroll=8)

    @pl.when((i == 0) & (j == 0))
    def _():
        issue(idx0_ref, 0, 0, tm)

    @pl.when(used & (j == 0))
    def _():
        pltpu.make_async_copy(h_hbm.at[pl.ds(0, tm), :], xbuf.at[slot], sem.at[slot]).wait()

        def cast(r0, rc):
            hb_ref[pl.ds(r0, rc), :] = xbuf[slot, pl.ds(r0, rc), :].astype(BF16)

        _for_row_chunks(tm, cast)

    @pl.when(i + 1 < nu)
    def _():
        issue(idxn_ref, 1 - slot, j * per, per)

    if rem:
        @pl.when((i + 1 < nu) & (j == 0))
        def _():
            issue(idxn_ref, 1 - slot, nsteps * per, rem)

    @pl.when(j == 0)
    def _():
        o_ref[...] = jnp.zeros_like(o_ref)

    @pl.when(used)
    def _():
        h = hb_ref[...]
        act = jax.nn.silu(_dot(h, wg_ref[0])) * _dot(h, wu_ref[0])
        o_ref[...] += _dot(act.astype(BF16), wd_ref[0])


def _moe_ffn(h, src, tile_expert, n_used, wg, wu, wd, *, tm, tf):
    ns = src.shape[0]
    d = h.shape[1]
    f = wg.shape[2]
    nt = ns // tm

    def w_up(i, j, te, nu):
        return (te[i], 0, jnp.where(i < nu[0], j, 0))

    def w_down(i, j, te, nu):
        return (te[i], jnp.where(i < nu[0], j, 0), 0)

    smem = pltpu.SMEM
    grid_spec = pltpu.PrefetchScalarGridSpec(
        num_scalar_prefetch=2,
        grid=(nt, f // tf),
        in_specs=[pl.BlockSpec((1, 1, tm), lambda i, j, te, nu: (0, 0, 0), memory_space=smem),
                  pl.BlockSpec((1, 1, tm), lambda i, j, te, nu: (jnp.minimum(i + 1, nt - 1), 0, 0),
                               memory_space=smem),
                  pl.BlockSpec(memory_space=pl.ANY),
                  pl.BlockSpec((1, d, tf), w_up),
                  pl.BlockSpec((1, d, tf), w_up),
                  pl.BlockSpec((1, tf, d), w_down)],
        out_specs=pl.BlockSpec((tm, d), lambda i, j, te, nu: (i, 0)),
        scratch_shapes=[pltpu.VMEM((2, tm, d), F32), pltpu.VMEM((tm, d), BF16),
                        pltpu.SemaphoreType.DMA((2,))])
    src3 = src.reshape(nt, 1, tm)
    return pl.pallas_call(
        functools.partial(_moe_ffn_kernel, tm=tm, nsteps=f // tf),
        grid_spec=grid_spec,
        out_shape=jax.ShapeDtypeStruct((ns, d), F32),
        compiler_params=_params(("arbitrary", "arbitrary")),
        name="moe_ffn",
    )(tile_expert, n_used, src3, src3, h, wg, wu, wd)


def _combine_kernel(s0f_ref, s1f_ref, s0n_ref, s1n_ref, x_ref, modl_ref, r_ref, fn_ref, y_ref, o_ref,
                    ya_ref, yb_ref, sem, *, cm):
    s = pl.program_id(0)
    nsteps = pl.num_programs(0)
    slot = s % 2

    def issue(i0_ref, i1_ref, sl):
        def body(r, carry):
            pltpu.make_async_copy(y_ref.at[pl.ds(i0_ref[0, 0, r], 1), :],
                                  ya_ref.at[sl, pl.ds(r, 1), :], sem.at[sl, 0]).start()
            pltpu.make_async_copy(y_ref.at[pl.ds(i1_ref[0, 0, r], 1), :],
                                  yb_ref.at[sl, pl.ds(r, 1), :], sem.at[sl, 1]).start()
            return carry

        lax.fori_loop(0, cm, body, 0, unroll=8)

    @pl.when(s == 0)
    def _():
        issue(s0f_ref, s1f_ref, 0)

    @pl.when(s + 1 < nsteps)
    def _():
        issue(s0n_ref, s1n_ref, 1 - slot)

    pltpu.make_async_copy(y_ref.at[pl.ds(0, cm), :], ya_ref.at[slot], sem.at[slot, 0]).wait()
    pltpu.make_async_copy(y_ref.at[pl.ds(0, cm), :], yb_ref.at[slot], sem.at[slot, 1]).wait()

    def body(r0, rc):
        rows = pl.ds(r0, rc)
        route = r_ref[rows, :]
        moe = route[:, 2:3] * ya_ref[slot, rows, :] + route[:, 3:4] * yb_ref[slot, rows, :]
        xn = x_ref[0, rows, :] + modl_ref[0, 5:6, :] * moe
        ms = jnp.mean(xn * xn, axis=-1, keepdims=True)
        o_ref[0, rows, :] = xn * lax.rsqrt(ms + EPS) * fn_ref[...]

    _for_row_chunks(cm, body)


def _combine(x, mods, route, final_gain, y, slot0, slot1, *, seq, cm):
    b, r, d = x.shape
    nt = seq // cm
    nsteps = b * nt
    smem = pltpu.SMEM

    def first():
        return pl.BlockSpec((1, 1, cm), lambda s: (0, 0, 0), memory_space=smem)

    def ahead():
        return pl.BlockSpec((1, 1, cm), lambda s: (jnp.minimum(s + 1, nsteps - 1), 0, 0),
                            memory_space=smem)

    s0 = slot0.reshape(nsteps, 1, cm)
    s1 = slot1.reshape(nsteps, 1, cm)
    return pl.pallas_call(
        functools.partial(_combine_kernel, cm=cm),
        grid=(nsteps,),
        in_specs=[first(), first(), ahead(), ahead(),
                  pl.BlockSpec((1, cm, d), lambda s: (s // nt, s % nt, 0)),
                  pl.BlockSpec((1, 6, d), lambda s: (s // nt, 0, 0)),
                  pl.BlockSpec((cm, LANES), lambda s: (s, 0)),
                  pl.BlockSpec((1, d), lambda s: (0, 0)),
                  pl.BlockSpec(memory_space=pl.ANY)],
        out_specs=pl.BlockSpec((1, cm, d), lambda s: (s // nt, s % nt, 0)),
        out_shape=jax.ShapeDtypeStruct((b, seq, d), F32),
        scratch_shapes=[pltpu.VMEM((2, cm, d), F32), pltpu.VMEM((2, cm, d), F32),
                        pltpu.SemaphoreType.DMA((2, 2))],
        compiler_params=_params(("arbitrary",)),
        name="moe_combine",
    )(s0, s1, s0, s1, x, mods, route, final_gain, y)


def _moe_plan(route, n_experts, tm):
    t = route.shape[0]
    eid = route[:, :TOP_K].astype(jnp.int32).reshape(-1)
    onehot = (eid[:, None] == jnp.arange(n_experts, dtype=jnp.int32)[None, :]).astype(jnp.int32)
    csum = jnp.cumsum(onehot, axis=0)
    rank = jnp.sum((csum - onehot) * onehot, axis=1)
    counts = csum[-1]
    padded = ((counts + tm - 1) // tm) * tm
    ends = jnp.cumsum(padded)
    starts = ends - padded
    slot = starts[eid] + rank
    ns = TOP_K * t + n_experts * tm
    token = jnp.arange(TOP_K * t, dtype=jnp.int32) // TOP_K
    src = jnp.zeros((ns,), jnp.int32).at[slot].set(token)
    tile_start = jnp.arange(ns // tm, dtype=jnp.int32) * tm
    tile_expert = jnp.minimum(jnp.sum((tile_start[:, None] >= ends[None, :]).astype(jnp.int32), axis=1),
                              n_experts - 1)
    n_used = (ends[-1] // tm).astype(jnp.int32).reshape(1)
    slots = slot.reshape(t, TOP_K)
    return src, tile_expert, n_used, slots[:, 0], slots[:, 1]


def _rope_tables(length):
    rows = length // GRID_W
    row = jnp.repeat(jnp.arange(rows, dtype=F32), GRID_W)
    col = jnp.tile(jnp.arange(GRID_W, dtype=F32), rows)
    n_freq = ATTN_HEAD_DIM // 4
    inv = ROPE_THETA ** (-jnp.arange(n_freq, dtype=F32) / n_freq)
    ar, ac = row[:, None] * inv, col[:, None] * inv
    cos = jnp.concatenate([jnp.cos(ar), jnp.cos(ar), jnp.cos(ac), jnp.cos(ac)], axis=1)
    sin = jnp.concatenate([-jnp.sin(ar), jnp.sin(ar), -jnp.sin(ac), jnp.sin(ac)], axis=1)
    reps = LANES // ATTN_HEAD_DIM
    return jnp.tile(cos, (1, reps)), jnp.tile(sin, (1, reps))


def _lower_bound(raw, layer):
    p = jax.nn.softmax(raw.astype(F32), axis=0)
    return (jnp.cumsum(p, axis=0) - p[0])[layer]


def kernel(x, c, ctx, c_ctx, w_mod, b_mod, norm_mix, norm_ffn, w_in, hg_lb_fwd, hg_lb_bwd, hg_norm,
           attn_sink, w_branch_a, w_branch_b, w_out, ffn_w_gate, ffn_w_up, ffn_w_down,
           moe_router, moe_w_gate, moe_w_up, moe_w_down, final_norm):
    b, seq, d = x.shape
    ctx_len = ctx.shape[1]
    depth = w_mod.shape[0]
    r = seq + ctx_len
    mix = d // 2
    kvw = (mix // ATTN_HEAD_DIM // ATTN_GROUP) * ATTN_HEAD_DIM
    n_experts = moe_router.shape[-1]
    assert depth == 2 and kvw % LANES == 0 and seq % ctx_len == 0
    assert seq % HG_CHUNK == 0 and ctx_len % HG_CHUNK == 0 and seq % GRID_W == 0

    tn = 512
    ga = -(-(6 * mix + 2 * kvw) // tn) * tn
    gb = ga + d
    n_in = 6 * mix + 2 * kvw
    tn_in = _pick_tile(gb + d, 768, 2 * LANES)
    tm_full = _pick_tile(r, 1152)
    tm_lat = _pick_tile(seq, 1024)
    tm_ffn = _pick_tile(r, 768)

    mb = -(-(b + 1) // 8) * 8
    cvec = jnp.zeros((mb, d), F32).at[:b].set(c).at[b].set(c_ctx)
    mods_all = _modulation(cvec, w_mod, b_mod)[:, :b + 1].reshape(depth, b + 1, 6, d)

    cos, sin = _rope_tables(seq)
    xs = jnp.concatenate([x, ctx], axis=1)

    for layer in range(depth):
        need_ctx = layer < depth - 1
        mods = mods_all[layer]
        w = w_in[layer]
        wp = jnp.concatenate([w[:, :n_in], jnp.zeros((d, ga - n_in), w.dtype), w[:, n_in:]],
                             axis=1).astype(BF16)
        p = _in_proj(xs, mods, norm_mix[layer].reshape(1, d), wp, seq=seq, tm=tm_full, tn=tn_in)

        lb_f = _lower_bound(hg_lb_fwd, layer).reshape(1, mix)
        lb_b = _lower_bound(hg_lb_bwd, layer).reshape(1, mix)
        o_b = _hgrn(p, lb_b, seq=seq, ctx_len=ctx_len, mix=mix, rev=True, fcol=2)
        a = _hgrn(p, lb_f, seq=seq, ctx_len=ctx_len, mix=mix, rev=False, fcol=1,
                  extra=(o_b, hg_norm[layer].reshape(1, mix)))

        attn, aux = _attention(p, cos, sin, attn_sink[layer], seq=seq, ctx_len=ctx_len, mix=mix, kvw=kvw)
        if need_ctx:
            attn = _context_attention(p, attn, attn_sink[layer], aux, seq=seq, ctx_len=ctx_len, mix=mix)

        tm, nt = (tm_full, r // tm_full) if need_ctx else (tm_lat, seq // tm_lat)
        y = _merge(a, attn, p, w_branch_a[layer].astype(BF16), w_branch_b[layer].astype(BF16),
                   tm=tm, ntiles=nt, tn=tn, ga_col=ga // tn, gb_col=gb // tn)
        xs = _out_proj(y, w_out[layer].astype(BF16), xs, mods, seq=seq, tm=tm, ntiles=nt, tn=tn)

        i = layer // 2
        gain = norm_ffn[layer].reshape(1, d)
        if layer % 2 == 0:
            tf = _pick_tile(ffn_w_gate.shape[-1], 512, LANES)
            xs = _ffn(xs, mods, gain, ffn_w_gate[i].astype(BF16), ffn_w_up[i].astype(BF16),
                      ffn_w_down[i].astype(BF16), seq=seq, tm=tm_ffn, tf=tf)
        else:
            tme = _pick_tile(seq, 512)
            wr = jnp.zeros((d, LANES), F32).at[:, :n_experts].set(moe_router[i])
            h, route = _route(xs, mods, gain, wr, seq=seq, tm=tme, n_experts=n_experts)
            src, tile_expert, n_used, slot0, slot1 = _moe_plan(route, n_experts, tme)
            tf = _pick_tile(moe_w_gate.shape[-1], 1024, LANES)
            yg = _moe_ffn(h, src, tile_expert, n_used, moe_w_gate[i].astype(BF16),
                          moe_w_up[i].astype(BF16), moe_w_down[i].astype(BF16), tm=tme, tf=tf)
            out = _combine(xs, mods, route, final_norm.reshape(1, d), yg, slot0, slot1,
                           seq=seq, cm=_pick_tile(seq, 256))
    return out
```

```python
import functools

import jax
import jax.numpy as jnp
import numpy as np
from jax import lax
from jax.experimental import pallas as pl
from jax.experimental.pallas import tpu as pltpu

HG_HEAD_DIM = 128
ATTN_HEAD_DIM = 64
ATTN_GROUP = 8
ATTN_BLOCK = 128
GRID_W = 64
ROPE_THETA = 10000.0
EPS = 1e-6
TOP_K = 2
LANES = 128
HG_CHUNK = 128
HG_FINE = 8
VMEM_LIMIT = 56 * 1024 * 1024

F32 = jnp.float32
BF16 = jnp.bfloat16


def _pick_tile(n, cap, mult=16):
    best = None
    for t in range(mult, min(cap, n) + 1, mult):
        if n % t == 0:
            best = t
    assert best is not None, (n, cap)
    return best


def _batch_rows_per_step(b):
    return next(k for k in (8, 4, 2, 1) if b % k == 0)


def _params(sem, vmem=VMEM_LIMIT, **kw):
    return pltpu.CompilerParams(dimension_semantics=sem, vmem_limit_bytes=vmem, **kw)


def _dot(a, b):
    return jnp.dot(a, b, preferred_element_type=F32)


def _dot_nt(a, b):
    return lax.dot_general(a, b, (((1,), (1,)), ((), ())), preferred_element_type=F32)


def _dot_tn(a, b):
    return lax.dot_general(a, b, (((0,), (0,)), ((), ())), preferred_element_type=F32)


def _norm_mod(x, gain, shift, scale):
    ms = jnp.mean(x * x, axis=-1, keepdims=True)
    return x * lax.rsqrt(ms + EPS) * gain * (1.0 + scale) + shift


def _is_ctx_rows(row0, n, seq):
    rows = row0 + lax.broadcasted_iota(jnp.int32, (n, 1), 0)
    return rows >= seq


def _mod_row(modl_ref, modc_ref, k, is_ctx):
    return jnp.where(is_ctx, modc_ref[0, k:k + 1, :], modl_ref[0, k:k + 1, :])


def _for_row_chunks(tm, body):
    rc = _pick_tile(tm, 32)

    def step(k, carry):
        body(pl.multiple_of(k * rc, rc), rc)
        return carry

    lax.fori_loop(0, tm // rc, step, 0)


def _norm_mod_rows(x_ref, h_ref, g_ref, modl_ref, modc_ref, k_shift, tile_row0, tm, seq):
    def body(r0, rc):
        rows = pl.ds(r0, rc)
        is_ctx = _is_ctx_rows(tile_row0 + r0, rc, seq)
        h = _norm_mod(x_ref[0, rows, :], g_ref[...], _mod_row(modl_ref, modc_ref, k_shift, is_ctx),
                      _mod_row(modl_ref, modc_ref, k_shift + 1, is_ctx))
        h_ref[rows, :] = h.astype(h_ref.dtype)

    _for_row_chunks(tm, body)


def _mod_kernel(c_ref, w_ref, b_ref, o_ref):
    act = jax.nn.silu(c_ref[...])
    o_ref[0] = jnp.dot(act, w_ref[0], preferred_element_type=F32,
                       precision=lax.Precision.HIGHEST) + b_ref[0]


def _modulation(cvec, w_mod, b_mod):
    depth, d, n6 = w_mod.shape
    mb = cvec.shape[0]
    tn = _pick_tile(n6, 1024, LANES)
    return pl.pallas_call(
        _mod_kernel,
        grid=(depth, n6 // tn),
        in_specs=[pl.BlockSpec((mb, d), lambda l, j: (0, 0)),
                  pl.BlockSpec((1, d, tn), lambda l, j: (l, 0, j)),
                  pl.BlockSpec((1, 1, tn), lambda l, j: (l, 0, j))],
        out_specs=pl.BlockSpec((1, mb, tn), lambda l, j: (l, 0, j)),
        out_shape=jax.ShapeDtypeStruct((depth, mb, n6), F32),
        compiler_params=_params(("parallel", "arbitrary")),
        name="modulation",
    )(cvec, w_mod, b_mod.reshape(depth, 1, n6))


def _in_proj_kernel(x_ref, modl_ref, modc_ref, g_ref, w_ref, o_ref, h_ref, *, tm, seq):
    tile_row0 = pl.program_id(1) * tm

    @pl.when(pl.program_id(2) == 0)
    def _():
        _norm_mod_rows(x_ref, h_ref, g_ref, modl_ref, modc_ref, 0, tile_row0, tm, seq)

    o_ref[0] = _dot(h_ref[...], w_ref[...]).astype(o_ref.dtype)


def _in_proj(x, mods, gain, w, *, seq, tm, tn):
    b, r, d = x.shape
    npad = w.shape[1]
    nb = mods.shape[0] - 1
    return pl.pallas_call(
        functools.partial(_in_proj_kernel, tm=tm, seq=seq),
        grid=(b, r // tm, npad // tn),
        in_specs=[pl.BlockSpec((1, tm, d), lambda i, t, j: (i, t, 0)),
                  pl.BlockSpec((1, 6, d), lambda i, t, j: (i, 0, 0)),
                  pl.BlockSpec((1, 6, d), lambda i, t, j: (nb, 0, 0)),
                  pl.BlockSpec((1, d), lambda i, t, j: (0, 0)),
                  pl.BlockSpec((d, tn), lambda i, t, j: (0, j))],
        out_specs=pl.BlockSpec((1, tm, tn), lambda i, t, j: (i, t, j)),
        out_shape=jax.ShapeDtypeStruct((b, r, npad), BF16),
        scratch_shapes=[pltpu.VMEM((tm, d), BF16)],
        compiler_params=_params(("parallel", "parallel", "arbitrary")),
        name="in_proj",
    )(x, mods, mods, gain, w)


def _hgrn_tables(c, rev):
    nlev = c.bit_length() - 1
    t = np.arange(c)[:, None]
    u = np.arange(c)[None, :]
    fams = [u <= t]
    level = np.full((c, c), -1)
    for li in range(nlev):
        half = c >> (li + 1)
        blk = 2 * half
        start = (t // blk) * blk
        qside = (t % blk) >= half
        if half <= HG_FINE:
            fams.append(np.where(qside, (u >= start + half) & (u <= t), (u > t) & (u < start + half)))
        level = np.where((t // blk == u // blk) & qside & ((u % blk) < half), li, level)
    sums = np.concatenate(fams, axis=0).astype(np.float32)
    if rev:
        sums = sums.reshape(len(fams), c, c)[:, ::-1, ::-1].reshape(len(fams) * c, c)
        level = level[::-1, ::-1]
    return (jnp.asarray(np.tile(sums, (1, 2)), BF16), jnp.asarray(np.ascontiguousarray(level), jnp.int32))


def _hgrn_kernel(*refs, rev, chunk, nheads, final, nb):
    if final:
        q_ref, f_ref, v_ref, lb_ref, sums_ref, lvl_ref, ob_ref, g_ref, gain_ref, o_ref, st_ref = refs
    else:
        q_ref, f_ref, v_ref, lb_ref, sums_ref, lvl_ref, o_ref, st_ref = refs
    c = chunk
    hd = HG_HEAD_DIM

    @pl.when(pl.program_id(1) == 0)
    def _():
        st_ref[...] = jnp.zeros_like(st_ref)

    lb = lb_ref[...]
    rows = lax.broadcasted_iota(jnp.int32, (c, 1), 0)
    level = lvl_ref[...]
    scale = HG_HEAD_DIM ** -0.5
    last = 0 if rev else c - 1
    nlev = c.bit_length() - 1
    in_level = [level == li for li in range(nlev)]
    for bi in range(nb):
        _hgrn_chunk(bi, q_ref, f_ref, v_ref, lb, sums_ref, o_ref, st_ref,
                    (ob_ref, g_ref, gain_ref) if final else None,
                    rows, in_level, scale, last, nlev, c, hd, nheads, rev)


def _hgrn_chunk(bi, q_ref, f_ref, v_ref, lb, sums_ref, o_ref, st_ref, merge_refs,
                rows, in_level, scale, last, nlev, c, hd, nheads, rev):
    final = merge_refs is not None
    if final:
        ob_ref, g_ref, gain_ref = merge_refs
    f = lb + (1.0 - lb) * jax.nn.sigmoid(f_ref[bi].astype(F32))
    g2 = jnp.log2(f)
    g_hi = g2.astype(BF16)
    g_lo = (g2 - g_hi.astype(F32)).astype(BF16)
    expo = _dot(sums_ref[...], jnp.concatenate([g_hi, g_lo], axis=0))
    outs, states = [], []

    for h in range(nheads):
        sl = slice(h * hd, (h + 1) * hd)
        qh = q_ref[bi, :, sl].astype(F32) * scale
        kh = 1.0 - f[:, sl]
        vh = v_ref[bi, :, sl]
        bh = expo[0:c, sl]
        btot = bh[last:last + 1, :]
        eb = jnp.exp2(bh)
        st = st_ref[bi, h]

        scores = jnp.zeros((c, c), F32)
        fine = 0
        for li in range(nlev):
            half = c >> (li + 1)
            blk = 2 * half
            qside = ((rows & half) == 0) if rev else ((rows & half) != 0)
            if half > HG_FINE:
                ref0 = half if rev else half - 1
                bref = bh[ref0:ref0 + 1, :]
                for k in range(1, c // blk):
                    bref = jnp.where(rows >= k * blk, bh[k * blk + ref0:k * blk + ref0 + 1, :], bref)
                x = bh - bref
                x = jnp.where(qside, x, -x)
            else:
                fine += 1
                x = expo[fine * c:(fine + 1) * c, sl]
            z = (jnp.where(qside, qh, kh) * jnp.exp2(x)).astype(BF16)
            scores = jnp.where(in_level[li], _dot_nt(z, z), scores)

        lhs = jnp.concatenate([(qh * eb).astype(BF16), scores.astype(BF16)], axis=1)
        rhs = jnp.concatenate([st.T.astype(BF16), vh], axis=0)
        o = _dot(lhs, rhs) + jnp.sum(qh * kh, axis=-1, keepdims=True) * vh.astype(F32)

        khat = (kh * jnp.exp2(btot - bh)).astype(BF16)
        states.append(st * eb[last:last + 1, :] + _dot_tn(vh, khat))

        if final:
            osum = o + ob_ref[bi, :, sl]
            ms = jnp.mean(osum * osum, axis=-1, keepdims=True)
            o = (osum * lax.rsqrt(ms + EPS) * gain_ref[:, sl]
                 * jax.nn.silu(g_ref[bi, :, sl].astype(F32)))
        outs.append(o.astype(o_ref.dtype))

    o_ref[bi] = jnp.concatenate(outs, axis=1)
    st_ref[bi] = jnp.stack(states, axis=0)


def _hgrn(p, lb, *, seq, ctx_len, mix, rev, fcol, extra=None):
    b, r, _ = p.shape
    c = HG_CHUNK
    nl, nc = seq // c, ctx_len // c
    nheads = mix // HG_HEAD_DIM
    nb = _batch_rows_per_step(b)

    if rev:
        def order(n):
            return nl + nc - 1 - n
    else:
        def order(n):
            return jnp.where(n < nc, nl + n, n - nc)

    def pspec(col):
        return pl.BlockSpec((nb, c, mix), lambda i, n: (i, order(n), col))

    sums, level = _hgrn_tables(c, rev)
    in_specs = [pspec(0), pspec(fcol), pspec(3), pl.BlockSpec((1, mix), lambda i, n: (0, 0)),
                pl.BlockSpec(sums.shape, lambda i, n: (0, 0)),
                pl.BlockSpec(level.shape, lambda i, n: (0, 0))]
    args = [p, p, p, lb, sums, level]
    final = extra is not None
    if final:
        o_other, gain = extra
        in_specs += [pl.BlockSpec((nb, c, mix), lambda i, n: (i, order(n), 0)), pspec(4),
                     pl.BlockSpec((1, mix), lambda i, n: (0, 0))]
        args += [o_other, p, gain]
    return pl.pallas_call(
        functools.partial(_hgrn_kernel, rev=rev, chunk=c, nheads=nheads, final=final, nb=nb),
        grid=(b // nb, nl + nc),
        in_specs=in_specs,
        out_specs=pl.BlockSpec((nb, c, mix), lambda i, n: (i, order(n), 0)),
        out_shape=jax.ShapeDtypeStruct((b, r, mix), BF16 if final else F32),
        scratch_shapes=[pltpu.VMEM((nb, nheads, HG_HEAD_DIM, HG_HEAD_DIM), F32)],
        compiler_params=_params(("parallel", "arbitrary")),
        name="hgrn_bwd" if rev else "hgrn_fwd",
    )(*args)


def _rope(x, cos, sin):
    lane = lax.broadcasted_iota(jnp.int32, x.shape, 1)
    first = (lane % 32) < 16
    swapped = jnp.where(first, pltpu.roll(x, LANES - 16, 1), pltpu.roll(x, 16, 1))
    return x * cos + swapped * sin


def _attn_kernel(*refs, windowed, nblk, kvh, ctx_len, bi=0):
    if windowed:
        (q_ref, kp_ref, kc_ref, kn_ref, vp_ref, vc_ref, vn_ref, kx_ref, vx_ref,
         cq_ref, sq_ref, cp_ref, sp_ref, cn_ref, sn_ref, sink_ref, o_ref) = refs
    else:
        q_ref, kx_ref, vx_ref, sink_ref, o_ref = refs
    blk = ATTN_BLOCK
    hd = ATTN_HEAD_DIM
    gw = ATTN_GROUP * hd
    n = pl.program_id(1)
    log2e = 1.4426950408889634
    scale = ATTN_HEAD_DIM ** -0.5 * log2e

    if windowed:
        kwin = [_rope(kp_ref[bi].astype(F32), cp_ref[...], sp_ref[...]),
                _rope(kc_ref[bi].astype(F32), cq_ref[...], sq_ref[...]),
                _rope(kn_ref[bi].astype(F32), cn_ref[...], sn_ref[...])]
        vwin = [vp_ref[bi], vc_ref[bi], vn_ref[bi]]
        nkeys = ctx_len + 3 * blk
        qi = lax.broadcasted_iota(jnp.int32, (blk, nkeys), 0)
        kj = lax.broadcasted_iota(jnp.int32, (blk, nkeys), 1) - ctx_len
        valid = (kj < 0) | ((kj >= qi) & (kj <= qi + 2 * blk)
                            & ((kj >= blk) | (n > 0)) & ((kj < 2 * blk) | (n < nblk - 1)))

    for g in range(kvh):
        ks = slice(g * hd, (g + 1) * hd)
        k_parts = [kx_ref[bi, :, ks]]
        v_parts = [vx_ref[bi, :, ks]]
        if windowed:
            k_parts += [kw[:, ks] for kw in kwin]
            v_parts += [vw[:, ks] for vw in vwin]
        k_all = jnp.concatenate(k_parts, axis=0).astype(BF16)
        v_all = jnp.concatenate(v_parts, axis=0).astype(BF16)

        q_heads = []
        for c4 in range(gw // LANES):
            col = g * gw + c4 * LANES
            qc = q_ref[bi, :, col:col + LANES].astype(F32)
            if windowed:
                qc = _rope(qc, cq_ref[...], sq_ref[...])
            qc = qc * scale
            q_heads += [qc[:, :hd], qc[:, hd:]]
        q_stack = jnp.concatenate(q_heads, axis=0).astype(BF16)
        s_all = _dot_nt(q_stack, k_all)

        p_list, inv_list = [], []
        for r in range(ATTN_GROUP):
            s = s_all[r * blk:(r + 1) * blk, :]
            if windowed:
                s = jnp.where(valid, s, -jnp.inf)
            sink = sink_ref[g * ATTN_GROUP + r] * log2e
            m = jnp.maximum(jnp.max(s, axis=-1, keepdims=True), sink)
            p = jnp.exp2(s - m)
            denom = jnp.sum(p, axis=-1, keepdims=True) + jnp.exp2(sink - m)
            p_list.append(p.astype(BF16))
            inv_list.append(1.0 / denom)
        o_all = _dot(jnp.concatenate(p_list, axis=0), v_all)
        outs = [o_all[r * blk:(r + 1) * blk, :] * inv_list[r] for r in range(ATTN_GROUP)]
        o_ref[bi, :, g * gw:(g + 1) * gw] = jnp.concatenate(outs, axis=1).astype(o_ref.dtype)


def _attention(p, cos, sin, sink, *, seq, ctx_len, mix, kvw):
    b, r, _ = p.shape
    blk = ATTN_BLOCK
    nblk = seq // blk
    kvh = kvw // ATTN_HEAD_DIM
    kcol = 6 * mix // kvw
    ctx_blk = seq // ctx_len
    nb = _batch_rows_per_step(b)

    def kv(col, off):
        return pl.BlockSpec((nb, blk, kvw),
                            lambda i, n: (i, jnp.clip(n + off, 0, nblk - 1), col))

    def tab(off):
        return pl.BlockSpec((blk, LANES), lambda i, n: (jnp.clip(n + off, 0, nblk - 1), 0))

    smem = pl.BlockSpec(memory_space=pltpu.SMEM)
    ctx_k = pl.BlockSpec((1, ctx_len, kvw), lambda i, n: (i, ctx_blk, kcol))
    ctx_v = pl.BlockSpec((1, ctx_len, kvw), lambda i, n: (i, ctx_blk, kcol + 1))
    out_shape = jax.ShapeDtypeStruct((b, r, mix), BF16)

    ctx_kb = pl.BlockSpec((nb, ctx_len, kvw), lambda i, n: (i, ctx_blk, kcol))
    ctx_vb = pl.BlockSpec((nb, ctx_len, kvw), lambda i, n: (i, ctx_blk, kcol + 1))
    lat = pl.pallas_call(
        functools.partial(_attn_window_kernel, windowed=True, nblk=nblk, kvh=kvh, ctx_len=ctx_len),
        grid=(b // nb, r // blk),
        in_specs=[pl.BlockSpec((nb, blk, mix), lambda i, n: (i, jnp.minimum(n, nblk - 1), 5)),
                  kv(kcol, -1), kv(kcol, 0), kv(kcol, 1),
                  kv(kcol + 1, -1), kv(kcol + 1, 0), kv(kcol + 1, 1),
                  ctx_kb, ctx_vb,
                  tab(0), tab(0), tab(-1), tab(-1), tab(1), tab(1), smem],
        out_specs=pl.BlockSpec((nb, blk, mix), lambda i, n: (i, n, 0)),
        out_shape=out_shape,
        compiler_params=_params(("parallel", "arbitrary")),
        name="attn_window",
    )(p, p, p, p, p, p, p, p, p, cos, sin, cos, sin, cos, sin, sink)
    return lat, (ctx_k, ctx_v, smem, out_shape, kvh)


def _context_attention(p, attn_lat, sink, aux, *, seq, ctx_len, mix):
    ctx_k, ctx_v, smem, out_shape, kvh = aux
    b = p.shape[0]
    blk = ATTN_BLOCK
    base = seq // blk
    return pl.pallas_call(
        functools.partial(_attn_ctx_kernel, windowed=False, nblk=0, kvh=kvh, ctx_len=ctx_len),
        grid=(b, ctx_len // blk),
        in_specs=[pl.BlockSpec((1, blk, mix), lambda i, n: (i, base + n, 5)),
                  ctx_k, ctx_v, smem,
                  pl.BlockSpec(memory_space=pl.ANY)],
        out_specs=pl.BlockSpec((1, blk, mix), lambda i, n: (i, base + n, 0)),
        out_shape=out_shape,
        input_output_aliases={4: 0},
        compiler_params=_params(("parallel", "arbitrary")),
        name="attn_context",
    )(p, p, p, sink, attn_lat)


def _attn_ctx_kernel(q_ref, kx_ref, vx_ref, sink_ref, prev_ref, o_ref, **kw):
    del prev_ref
    _attn_kernel(q_ref, kx_ref, vx_ref, sink_ref, o_ref, **kw)


def _attn_window_kernel(*refs, nblk, **kw):
    n = pl.program_id(1)
    o_ref = refs[-1]

    @pl.when(n < nblk)
    def _():
        for bi in range(o_ref.shape[0]):
            _attn_kernel(*refs, nblk=nblk, bi=bi, **kw)

    @pl.when(n >= nblk)
    def _():
        o_ref[...] = jnp.zeros_like(o_ref)


def _merge_kernel(a_ref, t_ref, ga_ref, gb_ref, wa_ref, wb_ref, o_ref):
    ya = _dot(a_ref[0], wa_ref[...])
    yb = _dot(t_ref[0], wb_ref[...])
    y = (jax.nn.sigmoid(ga_ref[0].astype(F32)) * ya
         + jax.nn.sigmoid(gb_ref[0].astype(F32)) * yb)
    o_ref[0] = y.astype(o_ref.dtype)


def _merge(a, attn, p, wa, wb, *, tm, ntiles, tn, ga_col, gb_col):
    b, r, mix = a.shape
    d = wa.shape[1]
    return pl.pallas_call(
        _merge_kernel,
        grid=(b, ntiles, d // tn),
        in_specs=[pl.BlockSpec((1, tm, mix), lambda i, t, j: (i, t, 0)),
                  pl.BlockSpec((1, tm, mix), lambda i, t, j: (i, t, 0)),
                  pl.BlockSpec((1, tm, tn), lambda i, t, j: (i, t, ga_col + j)),
                  pl.BlockSpec((1, tm, tn), lambda i, t, j: (i, t, gb_col + j)),
                  pl.BlockSpec((mix, tn), lambda i, t, j: (0, j)),
                  pl.BlockSpec((mix, tn), lambda i, t, j: (0, j))],
        out_specs=pl.BlockSpec((1, tm, tn), lambda i, t, j: (i, t, j)),
        out_shape=jax.ShapeDtypeStruct((b, ntiles * tm, d), BF16),
        compiler_params=_params(("parallel", "parallel", "arbitrary")),
        name="branch_merge",
    )(a, attn, p, p, wa, wb)


def _out_proj_kernel(y_ref, w_ref, x_ref, modl_ref, modc_ref, o_ref, *, tm, seq):
    is_ctx = _is_ctx_rows(pl.program_id(1) * tm, tm, seq)
    gate = jnp.where(is_ctx, modc_ref[0, 2:3, :], modl_ref[0, 2:3, :])
    o_ref[0] = x_ref[0] + gate * _dot(y_ref[0], w_ref[...])


def _out_proj(y, w, x, mods, *, seq, tm, ntiles, tn):
    b, r, d = x.shape
    nb = mods.shape[0] - 1
    return pl.pallas_call(
        functools.partial(_out_proj_kernel, tm=tm, seq=seq),
        grid=(b, ntiles, d // tn),
        in_specs=[pl.BlockSpec((1, tm, d), lambda i, t, j: (i, t, 0)),
                  pl.BlockSpec((d, tn), lambda i, t, j: (0, j)),
                  pl.BlockSpec((1, tm, tn), lambda i, t, j: (i, t, j)),
                  pl.BlockSpec((1, 6, tn), lambda i, t, j: (i, 0, j)),
                  pl.BlockSpec((1, 6, tn), lambda i, t, j: (nb, 0, j))],
        out_specs=pl.BlockSpec((1, tm, tn), lambda i, t, j: (i, t, j)),
        out_shape=jax.ShapeDtypeStruct((b, ntiles * tm, d), F32),
        compiler_params=_params(("parallel", "parallel", "arbitrary")),
        name="out_proj",
    )(y, w, x, mods, mods)


def _ffn_kernel(x_ref, modl_ref, modc_ref, g_ref, wg_ref, wu_ref, wd_ref, o_ref, h_ref, acc_ref,
                *, tm, seq):
    j = pl.program_id(2)
    last = pl.num_programs(2) - 1
    tile_row0 = pl.program_id(1) * tm

    @pl.when(j == 0)
    def _():
        _norm_mod_rows(x_ref, h_ref, g_ref, modl_ref, modc_ref, 3, tile_row0, tm, seq)
        acc_ref[...] = jnp.zeros_like(acc_ref)

    h = h_ref[...]
    act = jax.nn.silu(_dot(h, wg_ref[...])) * _dot(h, wu_ref[...])
    acc_ref[...] += _dot(act.astype(BF16), wd_ref[...])

    @pl.when(j == last)
    def _():
        def body(r0, rc):
            rows = pl.ds(r0, rc)
            is_ctx = _is_ctx_rows(tile_row0 + r0, rc, seq)
            o_ref[0, rows, :] = (x_ref[0, rows, :]
                                 + _mod_row(modl_ref, modc_ref, 5, is_ctx) * acc_ref[rows, :])

        _for_row_chunks(tm, body)


def _ffn(x, mods, gain, wg, wu, wd, *, seq, tm, tf):
    b, r, d = x.shape
    f = wg.shape[1]
    nb = mods.shape[0] - 1
    return pl.pallas_call(
        functools.partial(_ffn_kernel, tm=tm, seq=seq),
        grid=(b, r // tm, f // tf),
        in_specs=[pl.BlockSpec((1, tm, d), lambda i, t, j: (i, t, 0)),
                  pl.BlockSpec((1, 6, d), lambda i, t, j: (i, 0, 0)),
                  pl.BlockSpec((1, 6, d), lambda i, t, j: (nb, 0, 0)),
                  pl.BlockSpec((1, d), lambda i, t, j: (0, 0)),
                  pl.BlockSpec((d, tf), lambda i, t, j: (0, j)),
                  pl.BlockSpec((d, tf), lambda i, t, j: (0, j)),
                  pl.BlockSpec((tf, d), lambda i, t, j: (j, 0))],
        out_specs=pl.BlockSpec((1, tm, d), lambda i, t, j: (i, t, 0)),
        out_shape=jax.ShapeDtypeStruct((b, r, d), F32),
        scratch_shapes=[pltpu.VMEM((tm, d), BF16), pltpu.VMEM((tm, d), F32)],
        compiler_params=_params(("parallel", "parallel", "arbitrary")),
        name="ffn_dense",
    )(x, mods, mods, gain, wg, wu, wd)


def _route_kernel(x_ref, modl_ref, g_ref, wr_ref, h_ref, r_ref, *, n_experts, tm):
    def body(r0, rc):
        rows = pl.ds(r0, rc)
        h = _norm_mod(x_ref[0, rows, :], g_ref[...], modl_ref[0, 3:4, :], modl_ref[0, 4:5, :])
        h_ref[rows, :] = h

    _for_row_chunks(tm, body)

    logits = jnp.dot(h_ref[...], wr_ref[...], preferred_element_type=F32,
                     precision=lax.Precision.HIGHEST)
    lane = lax.broadcasted_iota(jnp.int32, logits.shape, 1)
    lg = jnp.where(lane < n_experts, logits, -jnp.inf)
    m1 = jnp.max(lg, axis=-1, keepdims=True)
    i1 = jnp.min(jnp.where(lg == m1, lane, LANES), axis=-1, keepdims=True)
    lg2 = jnp.where(lane == i1, -jnp.inf, lg)
    m2 = jnp.max(lg2, axis=-1, keepdims=True)
    i2 = jnp.min(jnp.where(lg2 == m2, lane, LANES), axis=-1, keepdims=True)
    e2 = jnp.exp(m2 - m1)
    w1 = 1.0 / (1.0 + e2)
    w2 = e2 / (1.0 + e2)
    r_ref[...] = jnp.where(lane == 0, i1.astype(F32),
                           jnp.where(lane == 1, i2.astype(F32),
                                     jnp.where(lane == 2, w1, jnp.where(lane == 3, w2, 0.0))))


def _route(x, mods, gain, wr, *, seq, tm, n_experts):
    b, r, d = x.shape
    nt = seq // tm
    return pl.pallas_call(
        functools.partial(_route_kernel, n_experts=n_experts, tm=tm),
        grid=(b, nt),
        in_specs=[pl.BlockSpec((1, tm, d), lambda i, t: (i, t, 0)),
                  pl.BlockSpec((1, 6, d), lambda i, t: (i, 0, 0)),
                  pl.BlockSpec((1, d), lambda i, t: (0, 0)),
                  pl.BlockSpec((d, LANES), lambda i, t: (0, 0))],
        out_specs=[pl.BlockSpec((tm, d), lambda i, t: (i * nt + t, 0)),
                   pl.BlockSpec((tm, LANES), lambda i, t: (i * nt + t, 0))],
        out_shape=[jax.ShapeDtypeStruct((b * seq, d), F32),
                   jax.ShapeDtypeStruct((b * seq, LANES), F32)],
        compiler_params=_params(("parallel", "parallel")),
        name="moe_route",
    )(x, mods, gain, wr)


def _moe_ffn_kernel(te_ref, nu_ref, idx0_ref, idxn_ref, h_hbm, wg_ref, wu_ref, wd_ref, o_ref,
                    xbuf, hb_ref, sem, *, tm, nsteps):
    del te_ref
    i, j = pl.program_id(0), pl.program_id(1)
    nu = nu_ref[0]
    used = i < nu
    slot = i % 2
    per = tm // nsteps // 8 * 8
    rem = tm - per * nsteps

    def row_copy(idx_ref, s, r):
        return pltpu.make_async_copy(h_hbm.at[pl.ds(idx_ref[0, 0, r], 1), :],
                                     xbuf.at[s, pl.ds(r, 1), :], sem.at[s])

    def issue(idx_ref, s, lo, n):
        def body(k, carry):
            row_copy(idx_ref, s, lo + k).start()
            return carry

        lax.fori_loop(0, n, body, 0, unroll=8)

    @pl.when((i == 0) & (j == 0))
    def _():
        issue(idx0_ref, 0, 0, tm)

    @pl.when(used & (j == 0))
    def _():
        pltpu.make_async_copy(h_hbm.at[pl.ds(0, tm), :], xbuf.at[slot], sem.at[slot]).wait()

        def cast(r0, rc):
            hb_ref[pl.ds(r0, rc), :] = xbuf[slot, pl.ds(r0, rc), :].astype(BF16)

        _for_row_chunks(tm, cast)

    @pl.when(i + 1 < nu)
    def _():
        issue(idxn_ref, 1 - slot, j * per, per)

    if rem:
        @pl.when((i + 1 < nu) & (j == 0))
        def _():
            issue(idxn_ref, 1 - slot, nsteps * per, rem)

    @pl.when(j == 0)
    def _():
        o_ref[...] = jnp.zeros_like(o_ref)

    @pl.when(used)
    def _():
        h = hb_ref[...]
        act = jax.nn.silu(_dot(h, wg_ref[0])) * _dot(h, wu_ref[0])
        o_ref[...] += _dot(act.astype(BF16), wd_ref[0])


def _moe_ffn(h, src, tile_expert, n_used, wg, wu, wd, *, tm, tf):
    ns = src.shape[0]
    d = h.shape[1]
    f = wg.shape[2]
    nt = ns // tm

    def w_up(i, j, te, nu):
        return (te[i], 0, jnp.where(i < nu[0], j, 0))

    def w_down(i, j, te, nu):
        return (te[i], jnp.where(i < nu[0], j, 0), 0)

    smem = pltpu.SMEM
    grid_spec = pltpu.PrefetchScalarGridSpec(
        num_scalar_prefetch=2,
        grid=(nt, f // tf),
        in_specs=[pl.BlockSpec((1, 1, tm), lambda i, j, te, nu: (0, 0, 0), memory_space=smem),
                  pl.BlockSpec((1, 1, tm), lambda i, j, te, nu: (jnp.minimum(i + 1, nt - 1), 0, 0),
                               memory_space=smem),
                  pl.BlockSpec(memory_space=pl.ANY),
                  pl.BlockSpec((1, d, tf), w_up),
                  pl.BlockSpec((1, d, tf), w_up),
                  pl.BlockSpec((1, tf, d), w_down)],
        out_specs=pl.BlockSpec((tm, d), lambda i, j, te, nu: (i, 0)),
        scratch_shapes=[pltpu.VMEM((2, tm, d), F32), pltpu.VMEM((tm, d), BF16),
                        pltpu.SemaphoreType.DMA((2,))])
    src3 = src.reshape(nt, 1, tm)
    return pl.pallas_call(
        functools.partial(_moe_ffn_kernel, tm=tm, nsteps=f // tf),
        grid_spec=grid_spec,
        out_shape=jax.ShapeDtypeStruct((ns, d), F32),
        compiler_params=_params(("arbitrary", "arbitrary")),
        name="moe_ffn",
    )(tile_expert, n_used, src3, src3, h, wg, wu, wd)


def _combine_kernel(s0f_ref, s1f_ref, s0n_ref, s1n_ref, x_ref, modl_ref, r_ref, fn_ref, y_ref, o_ref,
                    ya_ref, yb_ref, sem, *, cm):
    s = pl.program_id(0)
    nsteps = pl.num_programs(0)
    slot = s % 2

    def issue(i0_ref, i1_ref, sl):
        def body(r, carry):
            pltpu.make_async_copy(y_ref.at[pl.ds(i0_ref[0, 0, r], 1), :],
                                  ya_ref.at[sl, pl.ds(r, 1), :], sem.at[sl, 0]).start()
            pltpu.make_async_copy(y_ref.at[pl.ds(i1_ref[0, 0, r], 1), :],
                                  yb_ref.at[sl, pl.ds(r, 1), :], sem.at[sl, 1]).start(priority=1)
            return carry

        lax.fori_loop(0, cm, body, 0, unroll=8)

    @pl.when(s == 0)
    def _():
        issue(s0f_ref, s1f_ref, 0)

    @pl.when(s + 1 < nsteps)
    def _():
        issue(s0n_ref, s1n_ref, 1 - slot)

    pltpu.make_async_copy(y_ref.at[pl.ds(0, cm), :], ya_ref.at[slot], sem.at[slot, 0]).wait()
    pltpu.make_async_copy(y_ref.at[pl.ds(0, cm), :], yb_ref.at[slot], sem.at[slot, 1]).wait()

    def body(r0, rc):
        rows = pl.ds(r0, rc)
        route = r_ref[rows, :]
        moe = route[:, 2:3] * ya_ref[slot, rows, :] + route[:, 3:4] * yb_ref[slot, rows, :]
        xn = x_ref[0, rows, :] + modl_ref[0, 5:6, :] * moe
        ms = jnp.mean(xn * xn, axis=-1, keepdims=True)
        o_ref[0, rows, :] = xn * lax.rsqrt(ms + EPS) * fn_ref[...]

    _for_row_chunks(cm, body)


def _combine(x, mods, route, final_gain, y, slot0, slot1, *, seq, cm):
    b, r, d = x.shape
    nt = seq // cm
    nsteps = b * nt
    smem = pltpu.SMEM

    def first():
        return pl.BlockSpec((1, 1, cm), lambda s: (0, 0, 0), memory_space=smem)

    def ahead():
        return pl.BlockSpec((1, 1, cm), lambda s: (jnp.minimum(s + 1, nsteps - 1), 0, 0),
                            memory_space=smem)

    s0 = slot0.reshape(nsteps, 1, cm)
    s1 = slot1.reshape(nsteps, 1, cm)
    return pl.pallas_call(
        functools.partial(_combine_kernel, cm=cm),
        grid=(nsteps,),
        in_specs=[first(), first(), ahead(), ahead(),
                  pl.BlockSpec((1, cm, d), lambda s: (s // nt, s % nt, 0)),
                  pl.BlockSpec((1, 6, d), lambda s: (s // nt, 0, 0)),
                  pl.BlockSpec((cm, LANES), lambda s: (s, 0)),
                  pl.BlockSpec((1, d), lambda s: (0, 0)),
                  pl.BlockSpec(memory_space=pl.ANY)],
        out_specs=pl.BlockSpec((1, cm, d), lambda s: (s // nt, s % nt, 0)),
        out_shape=jax.ShapeDtypeStruct((b, seq, d), F32),
        scratch_shapes=[pltpu.VMEM((2, cm, d), F32), pltpu.VMEM((2, cm, d), F32),
                        pltpu.SemaphoreType.DMA((2, 2))],
        compiler_params=_params(("arbitrary",)),
        name="moe_combine",
    )(s0, s1, s0, s1, x, mods, route, final_gain, y)


def _moe_plan(route, n_experts, tm):
    t = route.shape[0]
    eid = route[:, :TOP_K].astype(jnp.int32).reshape(-1)
    onehot = (eid[:, None] == jnp.arange(n_experts, dtype=jnp.int32)[None, :]).astype(jnp.int32)
    csum = jnp.cumsum(onehot, axis=0)
    rank = jnp.sum((csum - onehot) * onehot, axis=1)
    counts = csum[-1]
    padded = ((counts + tm - 1) // tm) * tm
    ends = jnp.cumsum(padded)
    starts = ends - padded
    slot = starts[eid] + rank
    ns = TOP_K * t + n_experts * tm
    token = jnp.arange(TOP_K * t, dtype=jnp.int32) // TOP_K
    src = jnp.zeros((ns,), jnp.int32).at[slot].set(token)
    tile_start = jnp.arange(ns // tm, dtype=jnp.int32) * tm
    tile_expert = jnp.minimum(jnp.sum((tile_start[:, None] >= ends[None, :]).astype(jnp.int32), axis=1),
                              n_experts - 1)
    n_used = (ends[-1] // tm).astype(jnp.int32).reshape(1)
    slots = slot.reshape(t, TOP_K)
    return src, tile_expert, n_used, slots[:, 0], slots[:, 1]


def _rope_tables(length):
    rows = length // GRID_W
    row = jnp.repeat(jnp.arange(rows, dtype=F32), GRID_W)
    col = jnp.tile(jnp.arange(GRID_W, dtype=F32), rows)
    n_freq = ATTN_HEAD_DIM // 4
    inv = ROPE_THETA ** (-jnp.arange(n_freq, dtype=F32) / n_freq)
    ar, ac = row[:, None] * inv, col[:, None] * inv
    cos = jnp.concatenate([jnp.cos(ar), jnp.cos(ar), jnp.cos(ac), jnp.cos(ac)], axis=1)
    sin = jnp.concatenate([-jnp.sin(ar), jnp.sin(ar), -jnp.sin(ac), jnp.sin(ac)], axis=1)
    reps = LANES // ATTN_HEAD_DIM
    return jnp.tile(cos, (1, reps)), jnp.tile(sin, (1, reps))


def _lower_bound(raw, layer):
    p = jax.nn.softmax(raw.astype(F32), axis=0)
    return (jnp.cumsum(p, axis=0) - p[0])[layer]


def kernel(x, c, ctx, c_ctx, w_mod, b_mod, norm_mix, norm_ffn, w_in, hg_lb_fwd, hg_lb_bwd, hg_norm,
           attn_sink, w_branch_a, w_branch_b, w_out, ffn_w_gate, ffn_w_up, ffn_w_down,
           moe_router, moe_w_gate, moe_w_up, moe_w_down, final_norm):
    b, seq, d = x.shape
    ctx_len = ctx.shape[1]
    depth = w_mod.shape[0]
    r = seq + ctx_len
    mix = d // 2
    kvw = (mix // ATTN_HEAD_DIM // ATTN_GROUP) * ATTN_HEAD_DIM
    n_experts = moe_router.shape[-1]
    assert depth == 2 and kvw % LANES == 0 and seq % ctx_len == 0
    assert seq % HG_CHUNK == 0 and ctx_len % HG_CHUNK == 0 and seq % GRID_W == 0

    tn = 512
    ga = -(-(6 * mix + 2 * kvw) // tn) * tn
    gb = ga + d
    n_in = 6 * mix + 2 * kvw
    tn_in = _pick_tile(gb + d, 768, 2 * LANES)
    tm_full = _pick_tile(r, 1152)
    tm_lat = _pick_tile(seq, 1024)
    tm_ffn = _pick_tile(r, 768)

    mb = -(-(b + 1) // 8) * 8
    cvec = jnp.zeros((mb, d), F32).at[:b].set(c).at[b].set(c_ctx)
    mods_all = _modulation(cvec, w_mod, b_mod)[:, :b + 1].reshape(depth, b + 1, 6, d)

    cos, sin = _rope_tables(seq)
    xs = jnp.concatenate([x, ctx], axis=1)

    for layer in range(depth):
        need_ctx = layer < depth - 1
        mods = mods_all[layer]
        w = w_in[layer]
        wp = jnp.concatenate([w[:, :n_in], jnp.zeros((d, ga - n_in), w.dtype), w[:, n_in:]],
                             axis=1).astype(BF16)
        p = _in_proj(xs, mods, norm_mix[layer].reshape(1, d), wp, seq=seq, tm=tm_full, tn=tn_in)

        lb_f = _lower_bound(hg_lb_fwd, layer).reshape(1, mix)
        lb_b = _lower_bound(hg_lb_bwd, layer).reshape(1, mix)
        o_b = _hgrn(p, lb_b, seq=seq, ctx_len=ctx_len, mix=mix, rev=True, fcol=2)
        a = _hgrn(p, lb_f, seq=seq, ctx_len=ctx_len, mix=mix, rev=False, fcol=1,
                  extra=(o_b, hg_norm[layer].reshape(1, mix)))

        attn, aux = _attention(p, cos, sin, attn_sink[layer], seq=seq, ctx_len=ctx_len, mix=mix, kvw=kvw)
        if need_ctx:
            attn = _context_attention(p, attn, attn_sink[layer], aux, seq=seq, ctx_len=ctx_len, mix=mix)

        tm, nt = (tm_full, r // tm_full) if need_ctx else (tm_lat, seq // tm_lat)
        y = _merge(a, attn, p, w_branch_a[layer].astype(BF16), w_branch_b[layer].astype(BF16),
                   tm=tm, ntiles=nt, tn=tn, ga_col=ga // tn, gb_col=gb // tn)
        xs = _out_proj(y, w_out[layer].astype(BF16), xs, mods, seq=seq, tm=tm, ntiles=nt, tn=tn)

        i = layer // 2
        gain = norm_ffn[layer].reshape(1, d)
        if layer % 2 == 0:
            tf = _pick_tile(ffn_w_gate.shape[-1], 512, LANES)
            xs = _ffn(xs, mods, gain, ffn_w_gate[i].astype(BF16), ffn_w_up[i].astype(BF16),
                      ffn_w_down[i].astype(BF16), seq=seq, tm=tm_ffn, tf=tf)
        else:
            tme = _pick_tile(seq, 512)
            wr = jnp.zeros((d, LANES), F32).at[:, :n_experts].set(moe_router[i])
            h, route = _route(xs, mods, gain, wr, seq=seq, tm=tme, n_experts=n_experts)
            src, tile_expert, n_used, slot0, slot1 = _moe_plan(route, n_experts, tme)
            tf = _pick_tile(moe_w_gate.shape[-1], 1024, LANES)
            yg = _moe_ffn(h, src, tile_expert, n_used, moe_w_gate[i].astype(BF16),
                          moe_w_up[i].astype(BF16), moe_w_down[i].astype(BF16), tm=tme, tf=tf)
            out = _combine(xs, mods, route, final_norm.reshape(1, d), yg, slot0, slot1,
                           seq=seq, cm=_pick_tile(seq, 256))
    return out
```
